```python
import numpy as np
import jax
import jax.numpy as jnp
from jax import lax

D_MODEL = 1024
BATCH = 2
SEQ = 8192
DEPTH = 2
DEC_BATCH = 128
DEC_SEQ = 1
PAST_LEN = 16384
PAGE_SIZE = 128

HEAD_DIM = 64
MLA_HEADS = 8
MLA_Q_LORA = 256
MLA_KV_LORA = 128
MLA_NOPE = 64
MLA_ROPE = 32
MLA_V = 64
ROPE_THETA = 10000.0
FOX_HEADS = 4
FOX_KV_HEADS = 2
MOBA_HEADS = 4
MOBA_KV_HEADS = 2
MOBA_BLOCK = 256
MOBA_TOPK = 3
Q_BLOCK = 128
PLE_DIM = 256
FFN_HIDDEN = -(-(8 * D_MODEL) // (3 * 256)) * 256
MLA_OUT = MLA_HEADS * MLA_V
FOX_OUT = FOX_HEADS * HEAD_DIM
MOBA_OUT = MOBA_HEADS * HEAD_DIM
MIX_WIDTH = MLA_OUT + FOX_OUT + MOBA_OUT
IN_SPLITS = (MLA_Q_LORA, MLA_KV_LORA, MLA_ROPE,
             FOX_HEADS * HEAD_DIM, FOX_KV_HEADS * HEAD_DIM, FOX_KV_HEADS * HEAD_DIM, FOX_HEADS,
             MOBA_HEADS * HEAD_DIM, MOBA_KV_HEADS * HEAD_DIM, MOBA_KV_HEADS * HEAD_DIM)
IN_WIDTH = sum(IN_SPLITS)
NORM_EPS = 1e-6

kernel_name = 'hybrid_mla_fox_moba_step'


def _rmsnorm(x, g):
    xf = x.astype(jnp.float32)
    y = xf * lax.rsqrt(jnp.mean(xf * xf, axis=-1, keepdims=True) + NORM_EPS)
    return (y * g.astype(jnp.float32)).astype(x.dtype)


def _rope_tables(pos):
    inv = 1.0 / (ROPE_THETA ** (np.arange(0, MLA_ROPE, 2, dtype=np.float32) / MLA_ROPE))
    ang = pos.astype(jnp.float32)[:, None] * jnp.asarray(inv, jnp.float32)
    return jnp.cos(ang), jnp.sin(ang)


def _rope(x, cos, sin):
    xf = x.astype(jnp.float32)
    x1, x2 = jnp.split(xf, 2, axis=-1)
    return jnp.concatenate([x1 * cos - x2 * sin, x2 * cos + x1 * sin], axis=-1).astype(x.dtype)


def _alibi_slopes(n):
    return jnp.asarray(2.0 ** (-8.0 * np.arange(1, n + 1) / n), jnp.float32)


def _causal(qpos, kpos):
    return kpos[None, :] <= qpos[:, None]


def _blocks_to_seq(o):
    nb, b, qb = o.shape[:3]
    return jnp.swapaxes(o, 0, 1).reshape((b, nb * qb) + o.shape[3:])


def _gather_pages(cache, layer, page_table):
    g = cache[layer, page_table]
    return g.reshape((g.shape[0], g.shape[1] * g.shape[2]) + g.shape[3:])


def _project(h, lw, pos):
    b, t, _ = h.shape
    offs = np.cumsum(IN_SPLITS)[:-1].tolist()
    cq, ckv, kpe, fq, fk, fv, ff, mq, mk, mv = jnp.split(h @ lw['w_in'], offs, axis=-1)
    cos, sin = _rope_tables(pos)
    q = (_rmsnorm(cq, lw['mla_q_norm']) @ lw['w_mla_uq']).reshape(b, t, MLA_HEADS, MLA_NOPE + MLA_ROPE)
    q_pe = _rope(q[..., MLA_NOPE:], cos[None, :, None], sin[None, :, None])
    q_lat = jnp.einsum('bthn,chn->bthc', q[..., :MLA_NOPE], lw['w_mla_uk'])
    heads = lambda a, n: a.reshape(b, t, n, HEAD_DIM)
    return dict(
        q_lat=q_lat, q_pe=q_pe,
        ckv=_rmsnorm(ckv, lw['mla_kv_norm']), kpe=_rope(kpe, cos[None], sin[None]),
        fq=heads(fq, FOX_HEADS), fk=heads(fk, FOX_KV_HEADS), fv=heads(fv, FOX_KV_HEADS),
        logf=jax.nn.log_sigmoid((ff + lw['b_fox_f']).astype(jnp.float32)),
        mq=heads(mq, MOBA_HEADS), mk=heads(mk, MOBA_KV_HEADS), mv=heads(mv, MOBA_KV_HEADS))


def _mla_attend(q_lat, q_pe, ckv, kpe, qpos, kpos):
    scale = (MLA_NOPE + MLA_ROPE) ** -0.5
    s = (jnp.einsum('bthc,bsc->bhts', q_lat, ckv)
         + jnp.einsum('bthr,bsr->bhts', q_pe, kpe)).astype(jnp.float32) * scale
    s = jnp.where(_causal(qpos, kpos)[None, None], s, -jnp.inf)
    p = jax.nn.softmax(s, axis=-1).astype(ckv.dtype)
    return jnp.einsum('bhts,bsc->bthc', p, ckv)


def _fox_attend(q, k, v, cum_q, cum_k, qpos, kpos):
    b, tq = q.shape[:2]
    length = k.shape[1]
    grp = FOX_HEADS // FOX_KV_HEADS
    qg = q.reshape(b, tq, FOX_KV_HEADS, grp, HEAD_DIM)
    s = jnp.einsum('btgrd,bsgd->bgrts', qg, k).astype(jnp.float32).reshape(b, FOX_HEADS, tq, length)
    s = s * HEAD_DIM ** -0.5 + jnp.swapaxes(cum_q, 1, 2)[..., None] - jnp.swapaxes(cum_k, 1, 2)[:, :, None, :]
    s = jnp.where(_causal(qpos, kpos)[None, None], s, -jnp.inf)
    p = jax.nn.softmax(s, axis=-1).astype(v.dtype).reshape(b, FOX_KV_HEADS, grp, tq, length)
    return jnp.einsum('bgrts,bsgd->btgrd', p, v).reshape(b, tq, FOX_HEADS, HEAD_DIM)


def _moba_attend(q, qpos, kmean, k_blocks, v_blocks, k_own, v_own, kpos_own):
    b, tq = q.shape[:2]
    kvh = np.arange(MOBA_HEADS) // (MOBA_HEADS // MOBA_KV_HEADS)
    slopes = _alibi_slopes(MOBA_HEADS)
    scale = HEAD_DIM ** -0.5
    qblk = qpos // MOBA_BLOCK
    ko, vo = k_own[:, :, kvh], v_own[:, :, kvh]
    dist_own = (qpos[:, None] - kpos_own[None, :]).astype(jnp.float32)
    s_own = (jnp.einsum('bthd,bshd->bths', q, ko).astype(jnp.float32) * scale
             - slopes[None, None, :, None] * dist_own[None, :, None, :])
    own_mask = ((kpos_own[None, :] // MOBA_BLOCK) == qblk[:, None]) & _causal(qpos, kpos_own)
    s_own = jnp.where(own_mask[None, :, None, :], s_own, -jnp.inf)
    nb = kmean.shape[1]
    n_sel = min(MOBA_TOPK, nb)
    if n_sel == 0:
        p = jax.nn.softmax(s_own, axis=-1).astype(vo.dtype)
        return jnp.einsum('bths,bshd->bthd', p, vo)
    gate = jnp.einsum('bthd,bnhd->bthn', q, kmean[:, :, kvh]).astype(jnp.float32)
    gate = jnp.where((jnp.arange(nb)[None, :] < qblk[:, None])[None, :, None, :], gate, -jnp.inf)
    _, idx = lax.top_k(gate, n_sel)
    valid = jnp.arange(n_sel)[None, :] < qblk[:, None]
    bi = jnp.arange(b)[:, None, None, None]
    gi = jnp.asarray(kvh)[None, None, :, None]
    ks = k_blocks[bi, idx, :, gi]
    vs = v_blocks[bi, idx, :, gi]
    kpos_sel = idx[..., None] * MOBA_BLOCK + jnp.arange(MOBA_BLOCK)
    dist_sel = (qpos[None, :, None, None, None] - kpos_sel).astype(jnp.float32)
    s_sel = (jnp.einsum('bthd,bthjkd->bthjk', q, ks).astype(jnp.float32) * scale
             - slopes[None, None, :, None, None] * dist_sel)
    s_sel = jnp.where(valid[None, :, None, :, None], s_sel, -jnp.inf).reshape(b, tq, MOBA_HEADS, n_sel * MOBA_BLOCK)
    p = jax.nn.softmax(jnp.concatenate([s_sel, s_own], axis=-1), axis=-1).astype(vs.dtype)
    p_sel = p[..., :n_sel * MOBA_BLOCK].reshape(b, tq, MOBA_HEADS, n_sel, MOBA_BLOCK)
    p_own = p[..., n_sel * MOBA_BLOCK:]
    return jnp.einsum('bthjk,bthjkd->bthd', p_sel, vs) + jnp.einsum('bths,bshd->bthd', p_own, vo)


def _mix_out(o_mla, o_fox, o_moba, lw):
    b, t = o_mla.shape[:2]
    g_mla, g_fox, g_moba = jnp.split(lw['group_norm'], [MLA_OUT, MLA_OUT + FOX_OUT])
    o = jnp.concatenate([_rmsnorm(o_mla.reshape(b, t, MLA_OUT), g_mla),
                         _rmsnorm(o_fox.reshape(b, t, FOX_OUT), g_fox),
                         _rmsnorm(o_moba.reshape(b, t, MOBA_OUT), g_moba)], axis=-1)
    return o @ lw['w_out']


def _finish(x, mix, ple, lw):
    x = x + _rmsnorm(mix, lw['norm_mix_post'])
    h = _rmsnorm(x, lw['norm_ffn_pre'])
    f = (jax.nn.silu(h @ lw['w_ffn_gate']) * (h @ lw['w_ffn_up'])) @ lw['w_ffn_down']
    x = x + _rmsnorm(f, lw['norm_ffn_post'])
    return x + (ple @ lw['w_ple_proj']) * jax.nn.sigmoid(x @ lw['w_ple_gate'])


def _layer_prompt(x, ple, lw):
    b, s, _ = x.shape
    pos = jnp.arange(s)
    m = _project(_rmsnorm(x, lw['norm_mix_pre']), lw, pos)
    blocks = jnp.arange(s // Q_BLOCK)
    rows = lambda a, st: lax.dynamic_slice_in_dim(a, st, Q_BLOCK, axis=1)

    def mla_blk(i):
        st = i * Q_BLOCK
        return _mla_attend(rows(m['q_lat'], st), rows(m['q_pe'], st), m['ckv'], m['kpe'],
                           st + jnp.arange(Q_BLOCK), pos)
    o_mla = jnp.einsum('bthc,chv->bthv', _blocks_to_seq(lax.map(mla_blk, blocks)), lw['w_mla_uv'])

    cum = jnp.cumsum(m['logf'], axis=1)

    def fox_blk(i):
        st = i * Q_BLOCK
        return _fox_attend(rows(m['fq'], st), m['fk'], m['fv'], rows(cum, st), cum,
                           st + jnp.arange(Q_BLOCK), pos)
    o_fox = _blocks_to_seq(lax.map(fox_blk, blocks))

    nb = s // MOBA_BLOCK
    kb = m['mk'][:, :nb * MOBA_BLOCK].reshape(b, nb, MOBA_BLOCK, MOBA_KV_HEADS, HEAD_DIM)
    vb = m['mv'][:, :nb * MOBA_BLOCK].reshape(b, nb, MOBA_BLOCK, MOBA_KV_HEADS, HEAD_DIM)
    kmean = kb.astype(jnp.float32).mean(axis=2).astype(kb.dtype)
    pad = (-s) % MOBA_BLOCK
    k_pad = jnp.pad(m['mk'], ((0, 0), (0, pad), (0, 0), (0, 0)))
    v_pad = jnp.pad(m['mv'], ((0, 0), (0, pad), (0, 0), (0, 0)))

    def moba_blk(i):
        st = i * Q_BLOCK
        own = (st // MOBA_BLOCK) * MOBA_BLOCK
        return _moba_attend(rows(m['mq'], st), st + jnp.arange(Q_BLOCK), kmean, kb, vb,
                            lax.dynamic_slice_in_dim(k_pad, own, MOBA_BLOCK, axis=1),
                            lax.dynamic_slice_in_dim(v_pad, own, MOBA_BLOCK, axis=1),
                            own + jnp.arange(MOBA_BLOCK))
    o_moba = _blocks_to_seq(lax.map(moba_blk, blocks))

    y = _finish(x, _mix_out(o_mla, o_fox, o_moba, lw), ple, lw)
    return y, (m['ckv'], m['kpe'], m['fk'], m['fv'], m['logf'], m['mk'], m['mv'])


def _layer_sample(x, ple, lw, layer, caches, page_table):
    b, t, _ = x.shape
    length = PAST_LEN + t
    qpos = PAST_LEN + jnp.arange(t)
    kpos = jnp.arange(length)
    m = _project(_rmsnorm(x, lw['norm_mix_pre']), lw, qpos)
    new = (m['ckv'], m['kpe'], m['fk'], m['fv'], m['logf'], m['mk'], m['mv'])
    ckv, kpe, fk, fv, logf, mk, mv = [jnp.concatenate([_gather_pages(c, layer, page_table), n], axis=1)
                                      for c, n in zip(caches, new)]
    o_mla = jnp.einsum('bthc,chv->bthv', _mla_attend(m['q_lat'], m['q_pe'], ckv, kpe, qpos, kpos), lw['w_mla_uv'])
    cum = jnp.cumsum(logf.astype(jnp.float32), axis=1)
    o_fox = _fox_attend(m['fq'], fk, fv, cum[:, PAST_LEN:], cum, qpos, kpos)
    nb = length // MOBA_BLOCK
    kb = mk[:, :nb * MOBA_BLOCK].reshape(b, nb, MOBA_BLOCK, MOBA_KV_HEADS, HEAD_DIM)
    vb = mv[:, :nb * MOBA_BLOCK].reshape(b, nb, MOBA_BLOCK, MOBA_KV_HEADS, HEAD_DIM)
    kmean = kb.astype(jnp.float32).mean(axis=2).astype(kb.dtype)
    s0 = (PAST_LEN // MOBA_BLOCK) * MOBA_BLOCK
    o_moba = _moba_attend(m['mq'], qpos, kmean, kb, vb, mk[:, s0:], mv[:, s0:], jnp.arange(s0, length))
    y = _finish(x, _mix_out(o_mla, o_fox, o_moba, lw), ple, lw)
    return y, new


def setup_inputs(seed: int = 0) -> dict:
    key = jax.random.key(seed)
    ks = iter(jax.random.split(key, 40))
    nrm = lambda shape, scale=1.0: jax.random.normal(next(ks), shape, jnp.float32) * scale
    gain = lambda shape: 1.0 + nrm(shape, 0.05)
    n_pages = PAST_LEN // PAGE_SIZE
    n_used = DEC_BATCH * n_pages
    n_pool = n_used + n_used // 4
    return {
        'x_prompt': nrm((BATCH, SEQ, D_MODEL)),
        'x_sample': nrm((DEC_BATCH, DEC_SEQ, D_MODEL)),
        'cache_mla_ckv': nrm((DEPTH, n_pool, PAGE_SIZE, MLA_KV_LORA)),
        'cache_mla_kpe': nrm((DEPTH, n_pool, PAGE_SIZE, MLA_ROPE)),
        'cache_fox_k': nrm((DEPTH, n_pool, PAGE_SIZE, FOX_KV_HEADS, HEAD_DIM)),
        'cache_fox_v': nrm((DEPTH, n_pool, PAGE_SIZE, FOX_KV_HEADS, HEAD_DIM)),
        'cache_fox_logf': jax.nn.log_sigmoid(nrm((DEPTH, n_pool, PAGE_SIZE, FOX_HEADS))),
        'cache_moba_k': nrm((DEPTH, n_pool, PAGE_SIZE, MOBA_KV_HEADS, HEAD_DIM)),
        'cache_moba_v': nrm((DEPTH, n_pool, PAGE_SIZE, MOBA_KV_HEADS, HEAD_DIM)),
        'page_table': jax.random.permutation(next(ks), n_pool)[:n_used].reshape(DEC_BATCH, n_pages).astype(jnp.int32),
        'p_prompt': nrm((DEPTH, BATCH, SEQ, PLE_DIM)),
        'p_sample': nrm((DEPTH, DEC_BATCH, DEC_SEQ, PLE_DIM)),
        'norm_mix_pre': gain((DEPTH, D_MODEL)),
        'norm_mix_post': gain((DEPTH, D_MODEL)),
        'norm_ffn_pre': gain((DEPTH, D_MODEL)),
        'norm_ffn_post': gain((DEPTH, D_MODEL)),
        'w_in': nrm((DEPTH, D_MODEL, IN_WIDTH), D_MODEL ** -0.5),
        'b_fox_f': nrm((DEPTH, FOX_HEADS), 0.1),
        'mla_q_norm': gain((DEPTH, MLA_Q_LORA)),
        'mla_kv_norm': gain((DEPTH, MLA_KV_LORA)),
        'w_mla_uq': nrm((DEPTH, MLA_Q_LORA, MLA_HEADS * (MLA_NOPE + MLA_ROPE)), MLA_Q_LORA ** -0.5),
        'w_mla_uk': nrm((DEPTH, MLA_KV_LORA, MLA_HEADS, MLA_NOPE), MLA_KV_LORA ** -0.5),
        'w_mla_uv': nrm((DEPTH, MLA_KV_LORA, MLA_HEADS, MLA_V), MLA_KV_LORA ** -0.5),
        'group_norm': gain((DEPTH, MIX_WIDTH)),
        'w_out': nrm((DEPTH, MIX_WIDTH, D_MODEL), MIX_WIDTH ** -0.5),
        'w_ffn_gate': nrm((DEPTH, D_MODEL, FFN_HIDDEN), D_MODEL ** -0.5),
        'w_ffn_up': nrm((DEPTH, D_MODEL, FFN_HIDDEN), D_MODEL ** -0.5),
        'w_ffn_down': nrm((DEPTH, FFN_HIDDEN, D_MODEL), FFN_HIDDEN ** -0.5),
        'w_ple_proj': nrm((DEPTH, PLE_DIM, D_MODEL), PLE_DIM ** -0.5),
        'w_ple_gate': nrm((DEPTH, D_MODEL, D_MODEL), D_MODEL ** -0.5),
    }


def reference(x_prompt, x_sample, cache_mla_ckv, cache_mla_kpe, cache_fox_k, cache_fox_v,
              cache_fox_logf, cache_moba_k, cache_moba_v, page_table, p_prompt, p_sample,
              norm_mix_pre, norm_mix_post, norm_ffn_pre, norm_ffn_post, w_in, b_fox_f,
              mla_q_norm, mla_kv_norm, w_mla_uq, w_mla_uk, w_mla_uv, group_norm, w_out,
              w_ffn_gate, w_ffn_up, w_ffn_down, w_ple_proj, w_ple_gate):
    caches = (cache_mla_ckv, cache_mla_kpe, cache_fox_k, cache_fox_v, cache_fox_logf, cache_moba_k, cache_moba_v)
    y_prompt, y_sample = x_prompt, x_sample
    rows_p = [[] for _ in caches]
    rows_s = [[] for _ in caches]
    for i in range(DEPTH):
        lw = dict(norm_mix_pre=norm_mix_pre[i], norm_mix_post=norm_mix_post[i],
                  norm_ffn_pre=norm_ffn_pre[i], norm_ffn_post=norm_ffn_post[i],
                  w_in=w_in[i], b_fox_f=b_fox_f[i], mla_q_norm=mla_q_norm[i], mla_kv_norm=mla_kv_norm[i],
                  w_mla_uq=w_mla_uq[i], w_mla_uk=w_mla_uk[i], w_mla_uv=w_mla_uv[i],
                  group_norm=group_norm[i], w_out=w_out[i], w_ffn_gate=w_ffn_gate[i],
                  w_ffn_up=w_ffn_up[i], w_ffn_down=w_ffn_down[i],
                  w_ple_proj=w_ple_proj[i], w_ple_gate=w_ple_gate[i])
        y_prompt, st_p = _layer_prompt(y_prompt, p_prompt[i], lw)
        y_sample, st_s = _layer_sample(y_sample, p_sample[i], lw, i, caches, page_table)
        for lst, r in zip(rows_p, st_p):
            lst.append(r)
        for lst, r in zip(rows_s, st_s):
            lst.append(r)
    ckv_p, kpe_p, fk_p, fv_p, logf_p, mk_p, mv_p = [jnp.stack(a) for a in rows_p]
    ckv_s, kpe_s, fk_s, fv_s, logf_s, mk_s, mv_s = [jnp.stack(a) for a in rows_s]
    return (y_prompt, y_sample, ckv_p, ckv_s, kpe_p, kpe_s, fk_p, fk_s, fv_p, fv_s,
            logf_p, logf_s, mk_p, mk_s, mv_p, mv_s)
```

```python
import functools

import numpy as np
import jax
import jax.numpy as jnp
from jax import lax
from jax.experimental import pallas as pl
from jax.experimental.pallas import tpu as pltpu

HEAD_DIM = 64
MLA_HEADS = 8
MLA_Q_LORA = 256
MLA_KV_LORA = 128
MLA_NOPE = 64
MLA_ROPE = 32
MLA_V = 64
ROPE_THETA = 10000.0
FOX_HEADS = 4
FOX_KV_HEADS = 2
MOBA_HEADS = 4
MOBA_KV_HEADS = 2
MOBA_BLOCK = 256
MOBA_TOPK = 3
PAGE_SIZE = 128
NORM_EPS = 1e-6
MLA_QK = MLA_KV_LORA + MLA_ROPE

BF16 = jnp.bfloat16
F32 = jnp.float32
NEG = -1e30
VMEM_LIMIT = 56 * 1024 * 1024


def _nn(a, b):
    return jnp.dot(a, b, preferred_element_type=F32)


def _nt(a, b):
    return lax.dot_general(a, b, (((1,), (1,)), ((), ())), preferred_element_type=F32)


def _nt_f32(a, b):
    return lax.dot_general(a, b, (((1,), (1,)), ((), ())), precision=lax.Precision.HIGHEST,
                           preferred_element_type=F32)


def _nn_f32(a, b):
    return jnp.dot(a, b, precision=lax.Precision.HIGHEST, preferred_element_type=F32)


def _rms(x, g):
    return x * lax.rsqrt(jnp.mean(x * x, axis=-1, keepdims=True) + NORM_EPS) * g


def _split3(x):
    hi = x.astype(BF16)
    r = x - hi.astype(F32)
    mid = r.astype(BF16)
    lo = (r - mid.astype(F32)).astype(BF16)
    return hi, mid, lo


def _dot3(x, w):
    hi, mid, lo = _split3(x)
    return (_nn(hi, w) + _nn(mid, w)) + _nn(lo, w)


def _params(sem):
    return pltpu.CompilerParams(dimension_semantics=sem, vmem_limit_bytes=VMEM_LIMIT)


_Z_CQ, _Z_CKV, _Z_FQ, _Z_KPE, _Z_KPESW, _Z_END = 0, 256, 384, 640, 672, 704
_T_FK, _T_FV, _T_MV, _T_KPE, _T_KPESW, _T_FF, _T_END = 0, 128, 256, 384, 416, 448, 456


def _proj_kernel(x_ref, gpre_ref, wtm_ref, wtr_ref, wmq_ref, wmk_ref, bff_ref, gq_ref, gkv_ref, wuq_ref, wuk_ref,
                 cos_ref, sin_ref, cost_ref, sint_ref,
                 qmla_ref, kcat_ref, ckv_ref, kpet_ref, fq_ref, mq_ref,
                 fkt_ref, fvt_ref, mkt_ref, mvt_ref, logft_ref, cumt_ref, cumtm_ref, kmt_ref,
                 carry_ref, *, tq, nblk):
    t = pl.program_id(1)
    mla_scale = (MLA_NOPE + MLA_ROPE) ** -0.5
    qk_scale = HEAD_DIM ** -0.5

    x = x_ref[0]
    hf = _rms(x, gpre_ref[...])
    h = hf.astype(BF16)
    z = _nt(h, wtm_ref[...])
    zt = _nt(wtr_ref[...], h)
    zmq = _nt_f32(hf, wmq_ref[...])
    zmk = _nt_f32(wmk_ref[...], hf)

    cqn = _rms(z[:, _Z_CQ:_Z_CKV], gq_ref[...]).astype(BF16)
    q = _nn(cqn, wuq_ref[...])
    nrope = MLA_HEADS * MLA_ROPE
    nnope = MLA_HEADS * MLA_NOPE
    pe = (q[:, nnope:nnope + nrope] * cos_ref[...]
          + q[:, nnope + nrope:nnope + 2 * nrope] * sin_ref[...]) * mla_scale
    for j in range(MLA_HEADS // 2):
        lat2 = _nn(q[:, 128 * j:128 * (j + 1)].astype(BF16), wuk_ref[j]) * mla_scale
        qmla_ref[0, 2 * j, :, 0:MLA_KV_LORA] = lat2[:, 0:MLA_KV_LORA].astype(BF16)
        qmla_ref[0, 2 * j + 1, :, 0:MLA_KV_LORA] = lat2[:, MLA_KV_LORA:].astype(BF16)
    for hh in range(MLA_HEADS):
        qmla_ref[0, hh, :, MLA_KV_LORA:MLA_QK] = pe[:, MLA_ROPE * hh:MLA_ROPE * (hh + 1)].astype(BF16)

    ckv = _rms(z[:, _Z_CKV:_Z_FQ], gkv_ref[...])
    ckv_ref[0] = ckv
    kcat_ref[0, :, 0:MLA_KV_LORA] = ckv.astype(BF16)
    kpe = z[:, _Z_KPE:_Z_KPESW] * cos_ref[:, 0:MLA_ROPE] + z[:, _Z_KPESW:_Z_END] * sin_ref[:, 0:MLA_ROPE]
    kcat_ref[0, :, MLA_KV_LORA:MLA_QK] = kpe.astype(BF16)
    kpet_ref[0] = zt[_T_KPE:_T_KPESW] * cost_ref[...] + zt[_T_KPESW:_T_FF] * sint_ref[...]

    for hh in range(FOX_HEADS):
        fq_ref[0, hh] = (z[:, _Z_FQ + HEAD_DIM * hh:_Z_FQ + HEAD_DIM * (hh + 1)] * qk_scale).astype(BF16)
    for hh in range(MOBA_HEADS):
        mq_ref[0, hh] = zmq[:, HEAD_DIM * hh:HEAD_DIM * (hh + 1)] * qk_scale

    for g in range(FOX_KV_HEADS):
        fkt_ref[0, g] = zt[_T_FK + HEAD_DIM * g:_T_FK + HEAD_DIM * (g + 1)]
        fvt_ref[0, g] = zt[_T_FV + HEAD_DIM * g:_T_FV + HEAD_DIM * (g + 1)]
    for g in range(MOBA_KV_HEADS):
        mkt_ref[0, g] = zmk[HEAD_DIM * g:HEAD_DIM * (g + 1)]
        mvt_ref[0, g] = zt[_T_MV + HEAD_DIM * g:_T_MV + HEAD_DIM * (g + 1)]

    ff = zt[_T_FF:_T_END] + bff_ref[...]
    logf = jnp.minimum(ff, 0.0) - jnp.log1p(jnp.exp(-jnp.abs(ff)))
    logft_ref[0] = logf

    @pl.when(t == 0)
    def _():
        carry_ref[...] = jnp.zeros_like(carry_ref)
        kmt_ref[...] = jnp.zeros_like(kmt_ref)

    row = lax.broadcasted_iota(jnp.int32, (tq, tq), 0)
    col = lax.broadcasted_iota(jnp.int32, (tq, tq), 1)
    upper = (row <= col).astype(BF16)
    cum = _dot3(logf, upper) + carry_ref[:, 0:1]
    cumt_ref[0] = cum
    cumtm_ref[0] = jnp.transpose(cum)
    carry_ref[...] = jnp.broadcast_to(cum[:, tq - 1:tq], carry_ref.shape)

    nb_total = kmt_ref.shape[3]
    bw = tq // nblk
    lane = lax.broadcasted_iota(jnp.int32, (HEAD_DIM, nb_total), 1)
    for g in range(MOBA_KV_HEADS):
        cur = kmt_ref[0, g]
        for jj in range(nblk):
            blk = zmk[HEAD_DIM * g:HEAD_DIM * (g + 1), bw * jj:bw * (jj + 1)]
            mean = jnp.sum(blk, axis=1, keepdims=True) * (1.0 / bw)
            cur = jnp.where(lane == t * nblk + jj, mean, cur)
        kmt_ref[0, g] = cur


def _project(x, lw, tables, tq):
    bp, t, d = x.shape
    nt = t // tq
    nblk = max(tq // MOBA_BLOCK, 1)
    nb_total = nt * nblk
    cos, sin, cost, sint = tables
    const = lambda *shape: pl.BlockSpec(shape, lambda b, i: (0,) * len(shape))
    tok = lambda *shape: pl.BlockSpec((1, tq) + shape, lambda b, i: (b, i) + (0,) * len(shape))
    headmajor = lambda nh, w: pl.BlockSpec((1, nh, tq, w), lambda b, i: (b, 0, i, 0))
    trans = lambda nh, w: pl.BlockSpec((1, nh, w, tq), lambda b, i: (b, 0, 0, i))
    trans2 = lambda w: pl.BlockSpec((1, w, tq), lambda b, i: (b, 0, i))
    in_specs = [
        tok(d), const(1, d), const(_Z_END, d), const(_T_END, d),
        const(MOBA_HEADS * HEAD_DIM, d), const(MOBA_KV_HEADS * HEAD_DIM, d), const(8, 1),
        const(1, MLA_Q_LORA), const(1, MLA_KV_LORA), const(MLA_Q_LORA, 1024), const(MLA_HEADS // 2, 128, 256),
        pl.BlockSpec((tq, 256), lambda b, i: (i, 0)), pl.BlockSpec((tq, 256), lambda b, i: (i, 0)),
        pl.BlockSpec((MLA_ROPE, tq), lambda b, i: (0, i)), pl.BlockSpec((MLA_ROPE, tq), lambda b, i: (0, i)),
    ]
    out_shape = [
        jax.ShapeDtypeStruct((bp, MLA_HEADS, t, MLA_QK), BF16),
        jax.ShapeDtypeStruct((bp, t, MLA_QK), BF16),
        jax.ShapeDtypeStruct((bp, t, MLA_KV_LORA), F32),
        jax.ShapeDtypeStruct((bp, MLA_ROPE, t), F32),
        jax.ShapeDtypeStruct((bp, FOX_HEADS, t, HEAD_DIM), BF16),
        jax.ShapeDtypeStruct((bp, MOBA_HEADS, t, HEAD_DIM), F32),
        jax.ShapeDtypeStruct((bp, FOX_KV_HEADS, HEAD_DIM, t), F32),
        jax.ShapeDtypeStruct((bp, FOX_KV_HEADS, HEAD_DIM, t), F32),
        jax.ShapeDtypeStruct((bp, MOBA_KV_HEADS, HEAD_DIM, t), F32),
        jax.ShapeDtypeStruct((bp, MOBA_KV_HEADS, HEAD_DIM, t), F32),
        jax.ShapeDtypeStruct((bp, 8, t), F32),
        jax.ShapeDtypeStruct((bp, 8, t), F32),
        jax.ShapeDtypeStruct((bp, t, 8), F32),
        jax.ShapeDtypeStruct((bp, MOBA_KV_HEADS, HEAD_DIM, nb_total), F32),
    ]
    out_specs = [
        headmajor(MLA_HEADS, MLA_QK), tok(MLA_QK), tok(MLA_KV_LORA), trans2(MLA_ROPE),
        headmajor(FOX_HEADS, HEAD_DIM), headmajor(MOBA_HEADS, HEAD_DIM),
        trans(FOX_KV_HEADS, HEAD_DIM), trans(FOX_KV_HEADS, HEAD_DIM),
        trans(MOBA_KV_HEADS, HEAD_DIM), trans(MOBA_KV_HEADS, HEAD_DIM),
        trans2(8), trans2(8), tok(8),
        pl.BlockSpec((1, MOBA_KV_HEADS, HEAD_DIM, nb_total), lambda b, i: (b, 0, 0, 0)),
    ]
    return pl.pallas_call(
        functools.partial(_proj_kernel, tq=tq, nblk=nblk),
        grid=(bp, nt), in_specs=in_specs, out_specs=out_specs, out_shape=out_shape,
        scratch_shapes=[pltpu.VMEM((8, 128), F32)],
        compiler_params=_params(("parallel", "arbitrary")), name="proj",
    )(x, lw["g_pre"], lw["w_tm"], lw["w_tr"], lw["w_mq"], lw["w_mk"], lw["b_ff"], lw["g_q"], lw["g_kv"], lw["w_uq"], lw["w_uk"],
      cos, sin, cost, sint)


def _online_update(carry, s, pv):
    m, l, acc = carry
    m_new = jnp.maximum(m, jnp.max(s, axis=-1, keepdims=True))
    alpha = jnp.exp(m - m_new)
    p = jnp.exp(s - m_new)
    l = alpha * l + jnp.sum(p, axis=-1, keepdims=True)
    acc = alpha * acc + pv(p.astype(BF16))
    return m_new, l, acc


def _mla_attn_kernel(q_ref, k_ref, o_ref, *, tq, tk):
    i = pl.program_id(1)
    nh = q_ref.shape[1]
    rows = nh * tq
    q = q_ref[0].reshape(rows, MLA_QK)
    nfull = (i * tq) // tk
    nall = ((i + 1) * tq + tk - 1) // tk

    def step(j, carry, masked):
        off = pl.multiple_of(j * tk, tk)
        k = k_ref[0, pl.ds(off, tk), :]
        s = _nt(q, k)
        if masked:
            qpos = i * tq + (lax.broadcasted_iota(jnp.int32, (rows, tk), 0) & (tq - 1))
            kpos = off + lax.broadcasted_iota(jnp.int32, (rows, tk), 1)
            s = jnp.where(kpos <= qpos, s, NEG)
        return _online_update(carry, s, lambda p: _nn(p, k[:, 0:MLA_KV_LORA]))

    init = (jnp.full((rows, 1), NEG, F32), jnp.zeros((rows, 1), F32), jnp.zeros((rows, MLA_KV_LORA), F32))
    carry = lax.fori_loop(0, nfull, functools.partial(step, masked=False), init)
    _, l, acc = lax.fori_loop(nfull, nall, functools.partial(step, masked=True), carry)
    o = (acc / l).astype(o_ref.dtype)
    for hh in range(nh):
        o_ref[0, :, MLA_KV_LORA * hh:MLA_KV_LORA * (hh + 1)] = o[hh * tq:(hh + 1) * tq]


def _mla_attn(qmla, kcat, tq, tk):
    bp, nh, t, _ = qmla.shape
    return pl.pallas_call(
        functools.partial(_mla_attn_kernel, tq=tq, tk=tk),
        grid=(bp, t // tq),
        in_specs=[pl.BlockSpec((1, nh, tq, MLA_QK), lambda b, i: (b, 0, i, 0)),
                  pl.BlockSpec((1, t, MLA_QK), lambda b, i: (b, 0, 0))],
        out_specs=pl.BlockSpec((1, tq, nh * MLA_KV_LORA), lambda b, i: (b, i, 0)),
        out_shape=jax.ShapeDtypeStruct((bp, t, nh * MLA_KV_LORA), BF16),
        compiler_params=_params(("parallel", "arbitrary")), name="mla_attn",
    )(qmla, kcat)


def _cast_kv_once(i, kt_ref, vt_ref, kb_ref, vb_ref):
    @pl.when(i == 0)
    def _():
        kb_ref[...] = kt_ref[0].astype(BF16)
        vb_ref[...] = vt_ref[0].astype(BF16)


def _fox_attn_kernel(q_ref, kt_ref, vt_ref, cumt_ref, cumtm_ref, o_ref, kb_ref, vb_ref, *, tq, tk):
    i = pl.program_id(1)
    _cast_kv_once(i, kt_ref, vt_ref, kb_ref, vb_ref)
    nfull = (i * tq) // tk
    nall = ((i + 1) * tq + tk - 1) // tk
    grp = FOX_HEADS // FOX_KV_HEADS
    cum_q = cumtm_ref[0]
    for hh in range(FOX_HEADS):
        g = hh // grp
        q = q_ref[0, hh]
        cq = cum_q[:, hh:hh + 1]

        def step(j, carry, masked):
            off = pl.multiple_of(j * tk, tk)
            k = kb_ref[g, :, pl.ds(off, tk)]
            v = vb_ref[g, :, pl.ds(off, tk)]
            s = _nn(q, k) + (cq - cumt_ref[0, hh:hh + 1, pl.ds(off, tk)])
            if masked:
                qpos = i * tq + lax.broadcasted_iota(jnp.int32, (tq, tk), 0)
                kpos = off + lax.broadcasted_iota(jnp.int32, (tq, tk), 1)
                s = jnp.where(kpos <= qpos, s, NEG)
            return _online_update(carry, s, lambda p: _nt(p, v))

        init = (jnp.full((tq, 1), NEG, F32), jnp.zeros((tq, 1), F32), jnp.zeros((tq, HEAD_DIM), F32))
        carry = lax.fori_loop(0, nfull, functools.partial(step, masked=False), init)
        _, l, acc = lax.fori_loop(nfull, nall, functools.partial(step, masked=True), carry)
        o_ref[0, :, HEAD_DIM * hh:HEAD_DIM * (hh + 1)] = acc / l


def _fox_attn(fq, fkt, fvt, cumt, cumtm, tq, tk):
    bp, nh, t, _ = fq.shape
    kv_spec = pl.BlockSpec((1, FOX_KV_HEADS, HEAD_DIM, t), lambda b, i: (b, 0, 0, 0))
    return pl.pallas_call(
        functools.partial(_fox_attn_kernel, tq=tq, tk=tk),
        grid=(bp, t // tq),
        in_specs=[pl.BlockSpec((1, nh, tq, HEAD_DIM), lambda b, i: (b, 0, i, 0)), kv_spec, kv_spec,
                  pl.BlockSpec((1, 8, t), lambda b, i: (b, 0, 0)),
                  pl.BlockSpec((1, tq, 8), lambda b, i: (b, i, 0))],
        out_specs=pl.BlockSpec((1, tq, nh * HEAD_DIM), lambda b, i: (b, i, 0)),
        out_shape=jax.ShapeDtypeStruct((bp, t, nh * HEAD_DIM), F32),
        scratch_shapes=[pltpu.VMEM((FOX_KV_HEADS, HEAD_DIM, t), BF16)] * 2,
        compiler_params=_params(("parallel", "arbitrary")), name="fox_attn",
    )(fq, fkt, fvt, cumt, cumtm)


def _alibi_slope(hh, n):
    return float(2.0 ** (-8.0 * (hh + 1) / n))


def _top_blocks(gate, eligible, lane):
    g = jnp.where(eligible, gate, NEG)
    sel = jnp.zeros(g.shape, jnp.bool_)
    big = g.shape[-1]
    for _ in range(MOBA_TOPK):
        mx = jnp.max(g, axis=-1, keepdims=True)
        cand = (g == mx) & (g > 0.5 * NEG)
        idx = jnp.min(jnp.where(cand, lane, big), axis=-1, keepdims=True)
        pick = lane == idx
        sel = sel | pick
        g = jnp.where(pick, NEG, g)
    return sel


def _moba_attn_kernel(q_ref, kt_ref, vt_ref, kmt_ref, o_ref, kb_ref, vb_ref, *, tq):
    i = pl.program_id(1)
    _cast_kv_once(i, kt_ref, vt_ref, kb_ref, vb_ref)
    tk = tq
    nb = kmt_ref.shape[3]
    grp = MOBA_HEADS // MOBA_KV_HEADS
    lane = lax.broadcasted_iota(jnp.int32, (tq, nb), 1)
    rel = (lax.broadcasted_iota(jnp.int32, (tq, tk), 1)
           - lax.broadcasted_iota(jnp.int32, (tq, tk), 0)).astype(F32)
    for hh in range(MOBA_HEADS):
        g = hh // grp
        slope = _alibi_slope(hh, MOBA_HEADS)
        q32 = q_ref[0, hh]
        q = q32.astype(BF16)
        gate = _nn_f32(q32, kmt_ref[0, g])
        sel = _top_blocks(gate, lane < i, lane).astype(F32)
        bias0 = rel * slope

        def tile(j, carry, own):
            off = pl.multiple_of(j * tk, tk)
            k = kb_ref[g, :, pl.ds(off, tk)]
            v = vb_ref[g, :, pl.ds(off, tk)]
            s = _nn(q, k)
            if own:
                s = jnp.where(rel <= 0.0, s + bias0, NEG)
            else:
                picked = jnp.max(jnp.where(lane == j, sel, 0.0), axis=-1, keepdims=True) > 0.0
                shift = (j - i).astype(F32) * (slope * tk)
                s = jnp.where(picked, s + (bias0 + shift), NEG)
            return _online_update(carry, s, lambda p: _nt(p, v))

        init = (jnp.full((tq, 1), NEG, F32), jnp.zeros((tq, 1), F32), jnp.zeros((tq, HEAD_DIM), F32))
        carry = tile(i, init, True)
        _, l, acc = lax.fori_loop(0, i, functools.partial(tile, own=False), carry)
        o_ref[0, :, HEAD_DIM * hh:HEAD_DIM * (hh + 1)] = acc / l


def _moba_attn(mq, mkt, mvt, kmt):
    bp, nh, t, _ = mq.shape
    tq = MOBA_BLOCK
    nb = kmt.shape[3]
    kv_spec = pl.BlockSpec((1, MOBA_KV_HEADS, HEAD_DIM, t), lambda b, i: (b, 0, 0, 0))
    return pl.pallas_call(
        functools.partial(_moba_attn_kernel, tq=tq),
        grid=(bp, t // tq),
        in_specs=[pl.BlockSpec((1, nh, tq, HEAD_DIM), lambda b, i: (b, 0, i, 0)), kv_spec, kv_spec,
                  pl.BlockSpec((1, MOBA_KV_HEADS, HEAD_DIM, nb), lambda b, i: (b, 0, 0, 0))],
        out_specs=pl.BlockSpec((1, tq, nh * HEAD_DIM), lambda b, i: (b, i, 0)),
        out_shape=jax.ShapeDtypeStruct((bp, t, nh * HEAD_DIM), F32),
        scratch_shapes=[pltpu.VMEM((MOBA_KV_HEADS, HEAD_DIM, t), BF16)] * 2,
        compiler_params=_params(("parallel", "arbitrary")), name="moba_attn",
    )(mq, mkt, mvt, kmt)


class _PagedStream:
    def __init__(self, pt_ref, copies, n_pages, pp, reverse):
        self.pt_ref, self.copies, self.pp = pt_ref, copies, pp
        self.nc = n_pages // pp
        self.n_pages = n_pages
        self.reverse = reverse

    def chunk_of(self, step):
        seq = step // self.nc
        c = step % self.nc
        if self.reverse:
            c = self.nc - 1 - c
        return seq * self.n_pages + c * self.pp, c

    def _descs(self, step, slot):
        base, _ = self.chunk_of(step)
        out = []
        for pg in range(self.pp):
            out.extend(self.copies(self.pt_ref[base + pg], slot, pg))
        return out

    def start(self, step, slot):
        for d in self._descs(step, slot):
            d.start()

    def wait(self, step, slot):
        for d in self._descs(step, slot):
            d.wait()

    def run(self, body, init):
        b = pl.program_id(0)
        total = pl.num_programs(0) * self.nc

        @pl.when(b == 0)
        def _():
            self.start(0, 0)

        def loop(cc, carry):
            step = b * self.nc + cc
            slot = step % 2

            @pl.when(step + 1 < total)
            def _():
                self.start(step + 1, 1 - slot)

            self.wait(step, slot)
            _, c = self.chunk_of(step)
            return body(c, slot, carry)

        return lax.fori_loop(0, self.nc, loop, init)


def _mla_dec_kernel(pt_ref, q_ref, knew_ref, ckv_hbm, kpet_hbm, o_ref, ckv_buf, kpet_buf, sem,
                    *, layer, n_pages, pp):
    n = pp * PAGE_SIZE

    def copies(page, slot, pg):
        return [
            pltpu.make_async_copy(ckv_hbm.at[layer, page],
                                  ckv_buf.at[slot, pl.ds(pg * PAGE_SIZE, PAGE_SIZE)], sem.at[0, slot]),
            pltpu.make_async_copy(kpet_hbm.at[layer, page],
                                  kpet_buf.at[slot, :, pl.ds(pg * PAGE_SIZE, PAGE_SIZE)], sem.at[1, slot]),
        ]

    stream = _PagedStream(pt_ref, copies, n_pages, pp, reverse=False)
    q = q_ref[0]
    knew = knew_ref[0]
    s_new = jnp.sum(q.astype(F32) * knew.astype(F32), axis=-1, keepdims=True)
    v_new = jnp.broadcast_to(knew[:, 0:MLA_KV_LORA].astype(F32), (MLA_HEADS, MLA_KV_LORA))

    def body(c, slot, carry):
        kv = ckv_buf[slot].astype(BF16)
        kt = kpet_buf[slot].astype(BF16)
        s = _nt(q[:, 0:MLA_KV_LORA], kv) + _nn(q[:, MLA_KV_LORA:MLA_QK], kt)
        return _online_update(carry, s, lambda p: _nn(p, kv))

    init = (s_new, jnp.ones((MLA_HEADS, 1), F32), v_new)
    _, l, acc = stream.run(body, init)
    o_ref[0] = (acc / l).astype(o_ref.dtype)


def _mla_dec(pt, q, knew, cache_ckv, cache_kpet, layer, pp):
    db = q.shape[0]
    n_pages = pt.shape[0] // db
    n = pp * PAGE_SIZE
    grid_spec = pltpu.PrefetchScalarGridSpec(
        num_scalar_prefetch=1, grid=(db,),
        in_specs=[pl.BlockSpec((1, MLA_HEADS, MLA_QK), lambda b, pt: (b, 0, 0)),
                  pl.BlockSpec((1, 1, MLA_QK), lambda b, pt: (b, 0, 0)),
                  pl.BlockSpec(memory_space=pl.ANY), pl.BlockSpec(memory_space=pl.ANY)],
        out_specs=pl.BlockSpec((1, MLA_HEADS, MLA_KV_LORA), lambda b, pt: (b, 0, 0)),
        scratch_shapes=[pltpu.VMEM((2, n, MLA_KV_LORA), F32), pltpu.VMEM((2, MLA_ROPE, n), F32),
                        pltpu.SemaphoreType.DMA((2, 2))])
    return pl.pallas_call(
        functools.partial(_mla_dec_kernel, layer=layer, n_pages=n_pages, pp=pp),
        grid_spec=grid_spec, out_shape=jax.ShapeDtypeStruct((db, MLA_HEADS, MLA_KV_LORA), BF16),
        compiler_params=_params(("arbitrary",)), name="mla_dec",
    )(pt, q, knew, cache_ckv, cache_kpet)


def _rows_by_group(parts, grp):
    out = parts[-1]
    rowid = lax.broadcasted_iota(jnp.int32, out.shape, 0)
    for g in range(len(parts) - 2, -1, -1):
        out = jnp.where(rowid < (g + 1) * grp, parts[g], out)
    return out


def _fox_dec_kernel(pt_ref, q_ref, knew_ref, vnew_ref, lfnew_ref, kt_hbm, vt_hbm, lft_hbm, o_ref,
                    kt_buf, vt_buf, lf_buf, sem, *, layer, n_pages, pp):
    b = pl.program_id(0)
    grp = FOX_HEADS // FOX_KV_HEADS

    def copies(page, slot, pg):
        return [
            pltpu.make_async_copy(kt_hbm.at[layer, page],
                                  kt_buf.at[slot, :, :, pl.ds(pg * PAGE_SIZE, PAGE_SIZE)], sem.at[0, slot]),
            pltpu.make_async_copy(vt_hbm.at[layer, page],
                                  vt_buf.at[slot, :, :, pl.ds(pg * PAGE_SIZE, PAGE_SIZE)], sem.at[1, slot]),
            pltpu.make_async_copy(lft_hbm.at[layer, page],
                                  lf_buf.at[slot, pg, 0:FOX_HEADS], sem.at[2, slot]),
        ]

    @pl.when(b == 0)
    def _():
        lf_buf[...] = jnp.zeros_like(lf_buf)

    stream = _PagedStream(pt_ref, copies, n_pages, pp, reverse=True)
    q = q_ref[0]
    s_new = jnp.sum(q.astype(F32) * knew_ref[0], axis=-1, keepdims=True)
    row = lax.broadcasted_iota(jnp.int32, (PAGE_SIZE, PAGE_SIZE), 0)
    col = lax.broadcasted_iota(jnp.int32, (PAGE_SIZE, PAGE_SIZE), 1)
    later = (row > col).astype(BF16)

    def body(c, slot, carry):
        m, l, acc, run = carry
        lf = lf_buf[slot].reshape(pp * 8, PAGE_SIZE)
        within = _dot3(lf, later)
        tot = jnp.sum(lf, axis=-1, keepdims=True)
        bias = [None] * pp
        for pg in range(pp - 1, -1, -1):
            bias[pg] = within[8 * pg:8 * (pg + 1)] + run
            run = run + tot[8 * pg:8 * (pg + 1)]
        bias = jnp.concatenate(bias, axis=-1)
        kt = kt_buf[slot].astype(BF16)
        vt = vt_buf[slot].astype(BF16)
        s = _rows_by_group([_nn(q, kt[g]) for g in range(FOX_KV_HEADS)], grp) + bias
        pv = lambda p: _rows_by_group([_nt(p, vt[g]) for g in range(FOX_KV_HEADS)], grp)
        m, l, acc = _online_update((m, l, acc), s, pv)
        return m, l, acc, run

    init = (s_new, jnp.ones((8, 1), F32), vnew_ref[0], lfnew_ref[0])
    _, l, acc, _ = stream.run(body, init)
    o_ref[0] = acc / l


def _kv_dec_specs(db, with_lf):
    specs = [pl.BlockSpec((1, 8, HEAD_DIM), lambda b, pt: (b, 0, 0))] * 3
    if with_lf:
        specs.append(pl.BlockSpec((1, 8, 1), lambda b, pt: (b, 0, 0)))
    return specs


def _fox_dec(pt, q, knew, vnew, lfnew, cache_kt, cache_vt, cache_lft, layer, pp):
    db = q.shape[0]
    n_pages = pt.shape[0] // db
    n = pp * PAGE_SIZE
    grid_spec = pltpu.PrefetchScalarGridSpec(
        num_scalar_prefetch=1, grid=(db,),
        in_specs=_kv_dec_specs(db, True) + [pl.BlockSpec(memory_space=pl.ANY)] * 3,
        out_specs=pl.BlockSpec((1, 8, HEAD_DIM), lambda b, pt: (b, 0, 0)),
        scratch_shapes=[pltpu.VMEM((2, FOX_KV_HEADS, HEAD_DIM, n), F32),
                        pltpu.VMEM((2, FOX_KV_HEADS, HEAD_DIM, n), F32),
                        pltpu.VMEM((2, pp, 8, PAGE_SIZE), F32),
                        pltpu.SemaphoreType.DMA((3, 2))])
    return pl.pallas_call(
        functools.partial(_fox_dec_kernel, layer=layer, n_pages=n_pages, pp=pp),
        grid_spec=grid_spec, out_shape=jax.ShapeDtypeStruct((db, 8, HEAD_DIM), F32),
        compiler_params=_params(("arbitrary",)), name="fox_dec",
    )(pt, q, knew, vnew, lfnew, cache_kt, cache_vt, cache_lft)


def _moba_dec_kernel(pt_ref, q_ref, knew_ref, vnew_ref, kt_hbm, vt_hbm, o_ref,
                     kt_buf, vt_buf, ksum_ref, m_ref, l_ref, acc_ref, sem, *, layer, n_pages, pp):
    grp = MOBA_HEADS // MOBA_KV_HEADS
    n = pp * PAGE_SIZE
    bpc = n // MOBA_BLOCK
    nblocks = n_pages * PAGE_SIZE // MOBA_BLOCK
    past = n_pages * PAGE_SIZE

    def copies(page, slot, pg):
        return [
            pltpu.make_async_copy(kt_hbm.at[layer, page],
                                  kt_buf.at[slot, :, :, pl.ds(pg * PAGE_SIZE, PAGE_SIZE)], sem.at[0, slot]),
            pltpu.make_async_copy(vt_hbm.at[layer, page],
                                  vt_buf.at[slot, :, :, pl.ds(pg * PAGE_SIZE, PAGE_SIZE)], sem.at[1, slot]),
        ]

    stream = _PagedStream(pt_ref, copies, n_pages, pp, reverse=False)
    q32 = q_ref[0]
    q = q32.astype(BF16)
    rowid = lax.broadcasted_iota(jnp.int32, (8, 1), 0)
    slopes = jnp.zeros((8, 1), F32)
    for hh in range(MOBA_HEADS):
        slopes = jnp.where(rowid == hh, _alibi_slope(hh, MOBA_HEADS), slopes)
    lanepos = lax.broadcasted_iota(jnp.int32, (8, n), 1)
    blklane = lax.broadcasted_iota(jnp.int32, (HEAD_DIM, 128), 1)
    ksum_ref[...] = jnp.zeros_like(ksum_ref)

    def body(c, slot, carry):
        kt = kt_buf[slot].astype(BF16)
        vt = vt_buf[slot].astype(BF16)
        raw = _rows_by_group([_nn(q, kt[g]) for g in range(MOBA_KV_HEADS)], grp)
        dist = (past - c * n - lanepos).astype(F32)
        s = raw - slopes * dist
        for jj in range(bpc):
            lo, hi = jj * MOBA_BLOCK, (jj + 1) * MOBA_BLOCK
            blk = c * bpc + jj
            for g in range(MOBA_KV_HEADS):
                ksum = jnp.sum(kt_buf[slot, g, :, lo:hi], axis=-1, keepdims=True)
                ksum_ref[g] = jnp.where(blklane == blk, ksum, ksum_ref[g])
            sb = s[:, lo:hi]
            m = jnp.max(sb, axis=-1, keepdims=True)
            p = jnp.exp(sb - m)
            l = jnp.sum(p, axis=-1, keepdims=True)
            pb = p.astype(BF16)
            acc = _rows_by_group([_nt(pb, vt[g][:, lo:hi]) for g in range(MOBA_KV_HEADS)], grp)
            m_ref[blk] = jnp.broadcast_to(m, (8, 128))
            l_ref[blk] = jnp.broadcast_to(l, (8, 128))
            acc_ref[blk] = acc
        return carry

    stream.run(body, 0)

    gate = _rows_by_group([_nn_f32(q32, ksum_ref[g]) for g in range(MOBA_KV_HEADS)], grp)
    g = jnp.stack([jnp.broadcast_to(gate[:, bb:bb + 1], (8, 128)) for bb in range(nblocks)])
    blkid = lax.broadcasted_iota(jnp.int32, g.shape, 0)
    sel = jnp.zeros(g.shape, jnp.bool_)
    for _ in range(min(MOBA_TOPK, nblocks)):
        mx = jnp.max(g, axis=0, keepdims=True)
        cand = (g == mx) & (g > 0.5 * NEG)
        idx = jnp.min(jnp.where(cand, blkid, nblocks), axis=0, keepdims=True)
        pick = blkid == idx
        sel = sel | pick
        g = jnp.where(pick, NEG, g)
    s_new = jnp.sum(q.astype(F32) * knew_ref[0], axis=-1, keepdims=True)
    mb = m_ref[...]
    mtop = jnp.maximum(jnp.max(jnp.where(sel, mb, NEG), axis=0), s_new)
    w = jnp.where(sel, jnp.exp(mb - mtop[None]), 0.0)
    w_new = jnp.exp(s_new - mtop)
    denom = jnp.sum(w * l_ref[...], axis=0) + w_new
    num = jnp.sum(w[:, :, 0:HEAD_DIM] * acc_ref[...], axis=0) + w_new[:, 0:HEAD_DIM] * vnew_ref[0]
    o_ref[0] = num / denom[:, 0:HEAD_DIM]


def _moba_dec(pt, q, knew, vnew, cache_kt, cache_vt, layer, pp):
    db = q.shape[0]
    n_pages = pt.shape[0] // db
    n = pp * PAGE_SIZE
    nblocks = n_pages * PAGE_SIZE // MOBA_BLOCK
    assert nblocks <= 128
    grid_spec = pltpu.PrefetchScalarGridSpec(
        num_scalar_prefetch=1, grid=(db,),
        in_specs=_kv_dec_specs(db, False) + [pl.BlockSpec(memory_space=pl.ANY)] * 2,
        out_specs=pl.BlockSpec((1, 8, HEAD_DIM), lambda b, pt: (b, 0, 0)),
        scratch_shapes=[pltpu.VMEM((2, MOBA_KV_HEADS, HEAD_DIM, n), F32),
                        pltpu.VMEM((2, MOBA_KV_HEADS, HEAD_DIM, n), F32),
                        pltpu.VMEM((MOBA_KV_HEADS, HEAD_DIM, 128), F32), pltpu.VMEM((nblocks, 8, 128), F32),
                        pltpu.VMEM((nblocks, 8, 128), F32), pltpu.VMEM((nblocks, 8, HEAD_DIM), F32),
                        pltpu.SemaphoreType.DMA((2, 2))])
    return pl.pallas_call(
        functools.partial(_moba_dec_kernel, layer=layer, n_pages=n_pages, pp=pp),
        grid_spec=grid_spec, out_shape=jax.ShapeDtypeStruct((db, 8, HEAD_DIM), F32),
        compiler_params=_params(("arbitrary",)), name="moba_dec",
    )(pt, q, knew, vnew, cache_kt, cache_vt)


def _mix_kernel(olat_ref, ofox_ref, omoba_ref, x_ref, wuv_ref, gn_ref, wout_ref, gpost_ref, gffn_ref,
                x1_ref, h2_ref):
    n_mla = MLA_HEADS * MLA_V
    n_fox = FOX_HEADS * HEAD_DIM
    gn = gn_ref[...]
    o_mla = _nn(olat_ref[...], wuv_ref[...])
    o = jnp.concatenate([_rms(o_mla, gn[:, 0:n_mla]),
                         _rms(ofox_ref[...], gn[:, n_mla:n_mla + n_fox]),
                         _rms(omoba_ref[...], gn[:, n_mla + n_fox:])], axis=-1).astype(BF16)
    x1 = x_ref[...] + _rms(_nn(o, wout_ref[...]), gpost_ref[...])
    x1_ref[...] = x1
    h2_ref[...] = _rms(x1, gffn_ref[...]).astype(BF16)


def _mix(olat, ofox, omoba, x, lw, tq):
    t, d = x.shape
    row = lambda w: pl.BlockSpec((tq, w), lambda i: (i, 0))
    const = lambda *shape: pl.BlockSpec(shape, lambda i: (0,) * len(shape))
    return pl.pallas_call(
        _mix_kernel, grid=(t // tq,),
        in_specs=[row(olat.shape[1]), row(ofox.shape[1]), row(omoba.shape[1]), row(d),
                  const(*lw["w_uv"].shape), const(1, d), const(*lw["w_out"].shape), const(1, d), const(1, d)],
        out_specs=[row(d), row(d)],
        out_shape=[jax.ShapeDtypeStruct((t, d), F32), jax.ShapeDtypeStruct((t, d), BF16)],
        compiler_params=_params(("parallel",)), name="mix_out",
    )(olat, ofox, omoba, x, lw["w_uv"], lw["g_group"], lw["w_out"], lw["g_post"], lw["g_ffn_pre"])


def _ffn_kernel(x1_ref, h2_ref, ple_ref, wg_ref, wu_ref, wd_ref, gpost_ref, wpp_ref, wpg_ref, y_ref, acc_ref):
    j = pl.program_id(1)

    @pl.when(j == 0)
    def _():
        acc_ref[...] = jnp.zeros_like(acc_ref)

    h2 = h2_ref[...]
    gate = _nn(h2, wg_ref[...])
    act = (gate * jax.nn.sigmoid(gate)) * _nn(h2, wu_ref[...])
    acc_ref[...] += _nn(act.astype(BF16), wd_ref[...])

    @pl.when(j == pl.num_programs(1) - 1)
    def _():
        x2 = x1_ref[...] + _rms(acc_ref[...], gpost_ref[...])
        emb = _nn(ple_ref[...].astype(BF16), wpp_ref[...])
        y_ref[...] = x2 + emb * jax.nn.sigmoid(_nn(x2.astype(BF16), wpg_ref[...]))


def _ffn(x1, h2, ple, lw, tq, th):
    t, d = x1.shape
    hid = lw["w_gate"].shape[1]
    row = lambda w: pl.BlockSpec((tq, w), lambda i, j: (i, 0))
    const = lambda *shape: pl.BlockSpec(shape, lambda i, j: (0,) * len(shape))
    return pl.pallas_call(
        _ffn_kernel, grid=(t // tq, hid // th),
        in_specs=[row(d), row(d), row(ple.shape[1]),
                  pl.BlockSpec((d, th), lambda i, j: (0, j)), pl.BlockSpec((d, th), lambda i, j: (0, j)),
                  pl.BlockSpec((th, d), lambda i, j: (j, 0)),
                  const(1, d), const(*lw["w_ple_proj"].shape), const(*lw["w_ple_gate"].shape)],
        out_specs=row(d), out_shape=jax.ShapeDtypeStruct((t, d), F32),
        scratch_shapes=[pltpu.VMEM((tq, d), F32)],
        compiler_params=_params(("parallel", "arbitrary")), name="ffn",
    )(x1, h2, ple, lw["w_gate"], lw["w_up"], lw["w_down"], lw["g_ffn_post"], lw["w_ple_proj"], lw["w_ple_gate"])


def _rope_tables(pos):
    inv = 1.0 / (ROPE_THETA ** (np.arange(0, MLA_ROPE, 2, dtype=np.float32) / MLA_ROPE))
    ang = pos.astype(F32)[:, None] * jnp.asarray(inv, F32)
    c, s = jnp.cos(ang), jnp.sin(ang)
    cos32 = jnp.concatenate([c, c], axis=-1)
    sin32 = jnp.concatenate([-s, s], axis=-1)
    return (jnp.tile(cos32, (1, MLA_HEADS)), jnp.tile(sin32, (1, MLA_HEADS)),
            jnp.transpose(cos32), jnp.transpose(sin32))


def _layer_weights(i, w):
    d = w["w_in"].shape[1]
    wt = jnp.transpose(w["w_in"], (2, 0, 1))[:, i, :]
    o = np.cumsum([0, MLA_Q_LORA, MLA_KV_LORA, MLA_ROPE, FOX_HEADS * HEAD_DIM, FOX_KV_HEADS * HEAD_DIM,
                   FOX_KV_HEADS * HEAD_DIM, FOX_HEADS, MOBA_HEADS * HEAD_DIM, MOBA_KV_HEADS * HEAD_DIM,
                   MOBA_KV_HEADS * HEAD_DIM]).tolist()
    cq, ckv, kpe, fq, fk, fv, ff, mq, mk, mv = [wt[o[k]:o[k + 1]] for k in range(10)]
    half = MLA_ROPE // 2
    swap = np.concatenate([np.arange(half, MLA_ROPE), np.arange(0, half)])
    kpe_sw = kpe[swap]
    w_tm = jnp.concatenate([cq, ckv, fq, kpe, kpe_sw], axis=0).astype(BF16)
    w_tr = jnp.concatenate([fk, fv, mv, kpe, kpe_sw, ff, jnp.zeros((8 - FOX_HEADS, d), F32)], axis=0).astype(BF16)
    b_ff = jnp.concatenate([w["b_fox_f"][i], jnp.zeros((8 - FOX_HEADS,), F32)])[:, None]

    per = MLA_NOPE + MLA_ROPE
    heads = np.arange(MLA_HEADS)[:, None] * per
    nope_cols = (heads + np.arange(MLA_NOPE)[None]).reshape(-1)
    pe_cols = (heads + MLA_NOPE + np.arange(MLA_ROPE)[None]).reshape(-1)
    pesw_cols = (heads + MLA_NOPE + swap[None]).reshape(-1)
    w_uq = w["w_mla_uq"][i][:, np.concatenate([nope_cols, pe_cols, pesw_cols])].astype(BF16)

    uk = jnp.transpose(w["w_mla_uk"][i], (1, 2, 0))
    z = jnp.zeros((MLA_NOPE, MLA_KV_LORA), F32)
    w_uk = jnp.stack([jnp.concatenate([jnp.concatenate([uk[2 * j], z], axis=1),
                                       jnp.concatenate([z, uk[2 * j + 1]], axis=1)], axis=0)
                      for j in range(MLA_HEADS // 2)]).astype(BF16)
    uv = w["w_mla_uv"][i]
    w_uv = jnp.zeros((MLA_HEADS * MLA_KV_LORA, MLA_HEADS * MLA_V), F32)
    for hh in range(MLA_HEADS):
        w_uv = lax.dynamic_update_slice(w_uv, uv[:, hh, :], (hh * MLA_KV_LORA, hh * MLA_V))
    row = lambda a: a[i][None, :]
    return dict(
        g_pre=row(w["norm_mix_pre"]), w_tm=w_tm, w_tr=w_tr, w_mq=mq, w_mk=mk, b_ff=b_ff, g_q=row(w["mla_q_norm"]),
        g_kv=row(w["mla_kv_norm"]), w_uq=w_uq, w_uk=w_uk, w_uv=w_uv.astype(BF16),
        g_group=row(w["group_norm"]), w_out=w["w_out"][i].astype(BF16), g_post=row(w["norm_mix_post"]),
        g_ffn_pre=row(w["norm_ffn_pre"]), g_ffn_post=row(w["norm_ffn_post"]),
        w_gate=w["w_ffn_gate"][i].astype(BF16), w_up=w["w_ffn_up"][i].astype(BF16),
        w_down=w["w_ffn_down"][i].astype(BF16), w_ple_proj=w["w_ple_proj"][i].astype(BF16),
        w_ple_gate=w["w_ple_gate"][i].astype(BF16))


def _pick(n, prefs):
    for p in prefs:
        if n % p == 0:
            return p
    return n


def _cache_rows(m):
    kv = lambda a: jnp.transpose(a, (0, 3, 1, 2))
    return (m["ckv"], jnp.transpose(m["kpet"], (0, 2, 1)), kv(m["fkt"]), kv(m["fvt"]),
            jnp.transpose(m["logft"][:, 0:FOX_HEADS], (0, 2, 1)), kv(m["mkt"]), kv(m["mvt"]))


_PROJ_NAMES = ("qmla", "kcat", "ckv", "kpet", "fq", "mq", "fkt", "fvt", "mkt", "mvt", "logft", "cumt", "cumtm", "kmt")


def _transposed_caches(caches):
    ckv, kpe, fk, fv, lf, mk, mv = caches
    kv = lambda a: jnp.transpose(a, (0, 1, 3, 4, 2))
    return (ckv, jnp.transpose(kpe, (0, 1, 3, 2)), kv(fk), kv(fv), jnp.transpose(lf, (0, 1, 3, 2)), kv(mk), kv(mv))


def _sample_attention(ys, lw, layer, page_table, caches_t):
    cache_ckv, cache_kpet, cache_fkt, cache_fvt, cache_lft, cache_mkt, cache_mvt = caches_t
    db = ys.shape[1]
    n_pages = page_table.shape[1]
    pt_flat = page_table.reshape(-1)
    pp = _pick(n_pages, (16, 8, 4, 2))
    tables_s = _rope_tables(jnp.full((db,), n_pages * PAGE_SIZE, jnp.int32))
    pad_heads = lambda a: jnp.concatenate([a, jnp.zeros((db, 8 - a.shape[1]) + a.shape[2:], a.dtype)], axis=1)
    grp_rows = np.array([0, 0, 1, 1, 0, 0, 0, 0])
    tokmajor = lambda a: jnp.transpose(a[0], (2, 0, 1))

    ms = dict(zip(_PROJ_NAMES, _project(ys, lw, tables_s, db)))
    q_mla = jnp.transpose(ms["qmla"][0], (1, 0, 2))
    knew = ms["kcat"][0][:, None, :]
    o_lat = _mla_dec(pt_flat, q_mla, knew, cache_ckv, cache_kpet, layer, pp)
    fq = pad_heads(jnp.transpose(ms["fq"][0], (1, 0, 2)))
    lf_new = jnp.transpose(ms["logft"][0])[:, :, None]
    o_fox = _fox_dec(pt_flat, fq, tokmajor(ms["fkt"])[:, grp_rows], tokmajor(ms["fvt"])[:, grp_rows],
                     lf_new, cache_fkt, cache_fvt, cache_lft, layer, pp)
    mq = pad_heads(jnp.transpose(ms["mq"][0], (1, 0, 2)))
    o_moba = _moba_dec(pt_flat, mq, tokmajor(ms["mkt"])[:, grp_rows], tokmajor(ms["mvt"])[:, grp_rows],
                       cache_mkt, cache_mvt, layer, pp)
    return ms, o_lat, o_fox, o_moba


def kernel(x_prompt, x_sample, cache_mla_ckv, cache_mla_kpe, cache_fox_k, cache_fox_v, cache_fox_logf,
           cache_moba_k, cache_moba_v, page_table, p_prompt, p_sample, norm_mix_pre, norm_mix_post,
           norm_ffn_pre, norm_ffn_post, w_in, b_fox_f, mla_q_norm, mla_kv_norm, w_mla_uq, w_mla_uk,
           w_mla_uv, group_norm, w_out, w_ffn_gate, w_ffn_up, w_ffn_down, w_ple_proj, w_ple_gate):
    w = dict(norm_mix_pre=norm_mix_pre, norm_mix_post=norm_mix_post, norm_ffn_pre=norm_ffn_pre,
             norm_ffn_post=norm_ffn_post, w_in=w_in, b_fox_f=b_fox_f, mla_q_norm=mla_q_norm,
             mla_kv_norm=mla_kv_norm, w_mla_uq=w_mla_uq, w_mla_uk=w_mla_uk, w_mla_uv=w_mla_uv,
             group_norm=group_norm, w_out=w_out, w_ffn_gate=w_ffn_gate, w_ffn_up=w_ffn_up,
             w_ffn_down=w_ffn_down, w_ple_proj=w_ple_proj, w_ple_gate=w_ple_gate)
    depth = w_in.shape[0]
    bsz, seq, d = x_prompt.shape
    db = x_sample.shape[0]
    n_pages = page_table.shape[1]
    past = n_pages * PAGE_SIZE
    assert x_sample.shape[1] == 1 and seq % MOBA_BLOCK == 0 and past % MOBA_BLOCK == 0

    caches_t = _transposed_caches((cache_mla_ckv, cache_mla_kpe, cache_fox_k, cache_fox_v, cache_fox_logf,
                                   cache_moba_k, cache_moba_v))
    tables_p = _rope_tables(jnp.arange(seq))
    tq_proj = _pick(seq, (512, 256))
    tq_tail = _pick(seq, (512, 256, 128))
    th = _pick(w_ffn_gate.shape[2], (1408, 256, 128))
    tk_attn = _pick(seq, (512, 256))

    yp = x_prompt
    ys = x_sample.reshape(1, db, d)
    rows_p = [[] for _ in range(7)]
    rows_s = [[] for _ in range(7)]
    for i in range(depth):
        lw = _layer_weights(i, w)

        m = dict(zip(_PROJ_NAMES, _project(yp, lw, tables_p, tq_proj)))
        o_lat = _mla_attn(m["qmla"], m["kcat"], 128, tk_attn)
        o_fox = _fox_attn(m["fq"], m["fkt"], m["fvt"], m["cumt"], m["cumtm"], 256, tk_attn)
        o_moba = _moba_attn(m["mq"], m["mkt"], m["mvt"], m["kmt"])
        flat = lambda a: a.reshape(bsz * seq, a.shape[-1])
        x1, h2 = _mix(flat(o_lat), flat(o_fox), flat(o_moba), flat(yp), lw, tq_tail)
        yp = _ffn(x1, h2, flat(p_prompt[i]), lw, tq_tail, th).reshape(bsz, seq, d)
        for lst, r in zip(rows_p, _cache_rows(m)):
            lst.append(r)

        ms, o_lat_s, o_fox_s, o_moba_s = _sample_attention(ys, lw, i, page_table, caches_t)
        heads4 = lambda a: a[:, 0:4].reshape(db, 4 * HEAD_DIM)
        x1s, h2s = _mix(o_lat_s.reshape(db, MLA_HEADS * MLA_KV_LORA), heads4(o_fox_s), heads4(o_moba_s),
                        ys[0], lw, db)
        ys = _ffn(x1s, h2s, p_sample[i][:, 0, :], lw, db, th).reshape(1, db, d)
        for lst, r in zip(rows_s, _cache_rows(ms)):
            lst.append(jnp.transpose(r, (1, 0) + tuple(range(2, r.ndim))))

    outs = [yp, ys.reshape(db, 1, d)]
    for rp, rs in zip(rows_p, rows_s):
        outs.append(jnp.stack(rp))
        outs.append(jnp.stack(rs))
    return tuple(outs)
```

```python
import functools

import numpy as np
import jax
import jax.numpy as jnp
from jax import lax
from jax.experimental import pallas as pl
from jax.experimental.pallas import tpu as pltpu

HEAD_DIM = 64
MLA_HEADS = 8
MLA_Q_LORA = 256
MLA_KV_LORA = 128
MLA_NOPE = 64
MLA_ROPE = 32
MLA_V = 64
ROPE_THETA = 10000.0
FOX_HEADS = 4
FOX_KV_HEADS = 2
MOBA_HEADS = 4
MOBA_KV_HEADS = 2
MOBA_BLOCK = 256
MOBA_TOPK = 3
PAGE_SIZE = 128
NORM_EPS = 1e-6
MLA_QK = MLA_KV_LORA + MLA_ROPE
FOX_AUG = HEAD_DIM + 16
MOBA_AUG = 128
MOBA_MAX_BLOCKS = MOBA_AUG - HEAD_DIM - 4

BF16 = jnp.bfloat16
F32 = jnp.float32
NEG = -1e30
VMEM_LIMIT = 56 * 1024 * 1024

def _nn(a, b):
    return jnp.dot(a, b, preferred_element_type=F32)


def _nt(a, b):
    return lax.dot_general(a, b, (((1,), (1,)), ((), ())), preferred_element_type=F32)


def _nt_f32(a, b):
    return lax.dot_general(a, b, (((1,), (1,)), ((), ())), precision=lax.Precision.HIGHEST,
                           preferred_element_type=F32)


def _nn_f32(a, b):
    return jnp.dot(a, b, precision=lax.Precision.HIGHEST, preferred_element_type=F32)


def _rms(x, g):
    return x * lax.rsqrt(jnp.mean(x * x, axis=-1, keepdims=True) + NORM_EPS) * g


def _split3(x):
    hi = x.astype(BF16)
    r = x - hi.astype(F32)
    mid = r.astype(BF16)
    lo = (r - mid.astype(F32)).astype(BF16)
    return hi, mid, lo


def _dot3(x, w):
    hi, mid, lo = _split3(x)
    return (_nn(hi, w) + _nn(mid, w)) + _nn(lo, w)


def _params(sem):
    return pltpu.CompilerParams(dimension_semantics=sem, vmem_limit_bytes=VMEM_LIMIT)


_Z_CQ, _Z_CKV, _Z_FQ, _Z_KPE, _Z_KPESW, _Z_END = 0, 256, 384, 640, 672, 704
_T_FK, _T_FV, _T_MV, _T_KPE, _T_KPESW, _T_FF, _T_END = 0, 128, 256, 384, 416, 448, 456


def _proj_kernel(x_ref, gpre_ref, wtm_ref, wtr_ref, wmq_ref, wmk_ref, bff_ref, gq_ref, gkv_ref, wuq_ref, wuk_ref,
                 cos_ref, sin_ref, cost_ref, sint_ref,
                 qmla_ref, kcat_ref, ckv_ref, kpet_ref, fq_ref, mq_ref,
                 fkt_ref, fvt_ref, mkt_ref, mvt_ref, logft_ref, cumt_ref, kmt_ref,
                 carry_ref, *, tq, nblk):
    t = pl.program_id(1)
    mla_scale = (MLA_NOPE + MLA_ROPE) ** -0.5
    qk_scale = HEAD_DIM ** -0.5

    x = x_ref[0]
    hf = _rms(x, gpre_ref[...])
    h = hf.astype(BF16)
    z = _nt(h, wtm_ref[...])
    zt = _nt(wtr_ref[...], h)
    zmq = _nt_f32(hf, wmq_ref[...])
    zmk = _nt_f32(wmk_ref[...], hf)

    cqn = _rms(z[:, _Z_CQ:_Z_CKV], gq_ref[...]).astype(BF16)
    q = _nn(cqn, wuq_ref[...])
    nrope = MLA_HEADS * MLA_ROPE
    nnope = MLA_HEADS * MLA_NOPE
    pe = (q[:, nnope:nnope + nrope] * cos_ref[...]
          + q[:, nnope + nrope:nnope + 2 * nrope] * sin_ref[...]) * mla_scale
    for j in range(MLA_HEADS // 2):
        lat2 = _nn(q[:, 128 * j:128 * (j + 1)].astype(BF16), wuk_ref[j]) * mla_scale
        qmla_ref[0, 2 * j, :, 0:MLA_KV_LORA] = lat2[:, 0:MLA_KV_LORA].astype(BF16)
        qmla_ref[0, 2 * j + 1, :, 0:MLA_KV_LORA] = lat2[:, MLA_KV_LORA:].astype(BF16)
    for hh in range(MLA_HEADS):
        qmla_ref[0, hh, :, MLA_KV_LORA:MLA_QK] = pe[:, MLA_ROPE * hh:MLA_ROPE * (hh + 1)].astype(BF16)

    ckv = _rms(z[:, _Z_CKV:_Z_FQ], gkv_ref[...])
    ckv_ref[0] = ckv
    kcat_ref[0, :, 0:MLA_KV_LORA] = ckv.astype(BF16)
    kpe = z[:, _Z_KPE:_Z_KPESW] * cos_ref[:, 0:MLA_ROPE] + z[:, _Z_KPESW:_Z_END] * sin_ref[:, 0:MLA_ROPE]
    kcat_ref[0, :, MLA_KV_LORA:MLA_QK] = kpe.astype(BF16)
    kpet_ref[0] = zt[_T_KPE:_T_KPESW] * cost_ref[...] + zt[_T_KPESW:_T_FF] * sint_ref[...]

    ff = zt[_T_FF:_T_END] + bff_ref[...]
    logf = jnp.minimum(ff, 0.0) - jnp.log1p(jnp.exp(-jnp.abs(ff)))
    logft_ref[0] = logf

    @pl.when(t == 0)
    def _():
        carry_ref[...] = jnp.zeros_like(carry_ref)
        kmt_ref[...] = jnp.zeros_like(kmt_ref)

    row = lax.broadcasted_iota(jnp.int32, (tq, tq), 0)
    col = lax.broadcasted_iota(jnp.int32, (tq, tq), 1)
    upper = (row <= col).astype(BF16)
    cum = _dot3(logf, upper) + carry_ref[:, 0:1]
    cumt_ref[0] = cum
    cum_tm = jnp.transpose(cum)
    carry_ref[...] = jnp.broadcast_to(cum[:, tq - 1:tq], carry_ref.shape)

    lane = lax.broadcasted_iota(jnp.int32, (tq, 128), 1)
    for hh in range(FOX_HEADS):
        c1, c2, c3 = [c.astype(F32) for c in _split3(cum_tm[:, hh:hh + 1])]
        first = HEAD_DIM + 3 * (hh % (FOX_HEADS // FOX_KV_HEADS))
        extra = jnp.where((lane >= first) & (lane < first + 3), 1.0,
                          jnp.where(lane == HEAD_DIM + 6, c1,
                                    jnp.where(lane == HEAD_DIM + 7, c2, jnp.where(lane == HEAD_DIM + 8, c3, 0.0))))
        wide = z[:, _Z_FQ + HEAD_DIM * hh:_Z_FQ + HEAD_DIM * hh + 128] * qk_scale
        fq_ref[0, hh] = jnp.where(lane < HEAD_DIM, wide, extra)[:, 0:FOX_AUG].astype(BF16)
    for hh in range(MOBA_HEADS):
        wide = zmq[:, HEAD_DIM * hh:HEAD_DIM * hh + 128] * qk_scale
        mq_ref[0, hh] = jnp.where(lane < HEAD_DIM, wide, 0.0)

    for g in range(FOX_KV_HEADS):
        fkt_ref[0, g] = zt[_T_FK + HEAD_DIM * g:_T_FK + HEAD_DIM * (g + 1)]
        fvt_ref[0, g] = zt[_T_FV + HEAD_DIM * g:_T_FV + HEAD_DIM * (g + 1)]
    for g in range(MOBA_KV_HEADS):
        mkt_ref[0, g] = zmk[HEAD_DIM * g:HEAD_DIM * (g + 1)]
        mvt_ref[0, g] = zt[_T_MV + HEAD_DIM * g:_T_MV + HEAD_DIM * (g + 1)]

    bw = tq // nblk
    blane = lax.broadcasted_iota(jnp.int32, (HEAD_DIM, 128), 1)
    for g in range(MOBA_KV_HEADS):
        cur = kmt_ref[0, g]
        for jj in range(nblk):
            blk = zmk[HEAD_DIM * g:HEAD_DIM * (g + 1), bw * jj:bw * (jj + 1)]
            mean = jnp.sum(blk, axis=1, keepdims=True) * (1.0 / bw)
            cur = jnp.where(blane == t * nblk + jj, mean, cur)
        kmt_ref[0, g] = cur


def _project(x, lw, tables, tq):
    bp, t, d = x.shape
    nt = t // tq
    nblk = max(tq // MOBA_BLOCK, 1)
    assert nt * nblk <= MOBA_MAX_BLOCKS
    cos, sin, cost, sint = tables
    const = lambda *shape: pl.BlockSpec(shape, lambda b, i: (0,) * len(shape))
    tok = lambda *shape: pl.BlockSpec((1, tq) + shape, lambda b, i: (b, i) + (0,) * len(shape))
    headmajor = lambda nh, w: pl.BlockSpec((1, nh, tq, w), lambda b, i: (b, 0, i, 0))
    trans = lambda nh, w: pl.BlockSpec((1, nh, w, tq), lambda b, i: (b, 0, 0, i))
    trans2 = lambda w: pl.BlockSpec((1, w, tq), lambda b, i: (b, 0, i))
    in_specs = [
        tok(d), const(1, d), const(_Z_END, d), const(_T_END, d),
        const(MOBA_HEADS * HEAD_DIM + HEAD_DIM, d), const(MOBA_KV_HEADS * HEAD_DIM, d), const(8, 1),
        const(1, MLA_Q_LORA), const(1, MLA_KV_LORA), const(MLA_Q_LORA, 1024), const(MLA_HEADS // 2, 128, 256),
        pl.BlockSpec((tq, 256), lambda b, i: (i, 0)), pl.BlockSpec((tq, 256), lambda b, i: (i, 0)),
        pl.BlockSpec((MLA_ROPE, tq), lambda b, i: (0, i)), pl.BlockSpec((MLA_ROPE, tq), lambda b, i: (0, i)),
    ]
    out_shape = [
        jax.ShapeDtypeStruct((bp, MLA_HEADS, t, MLA_QK), BF16),
        jax.ShapeDtypeStruct((bp, t, MLA_QK), BF16),
        jax.ShapeDtypeStruct((bp, t, MLA_KV_LORA), F32),
        jax.ShapeDtypeStruct((bp, MLA_ROPE, t), F32),
        jax.ShapeDtypeStruct((bp, FOX_HEADS, t, FOX_AUG), BF16),
        jax.ShapeDtypeStruct((bp, MOBA_HEADS, t, MOBA_AUG), F32),
        jax.ShapeDtypeStruct((bp, FOX_KV_HEADS, HEAD_DIM, t), F32),
        jax.ShapeDtypeStruct((bp, FOX_KV_HEADS, HEAD_DIM, t), F32),
        jax.ShapeDtypeStruct((bp, MOBA_KV_HEADS, HEAD_DIM, t), F32),
        jax.ShapeDtypeStruct((bp, MOBA_KV_HEADS, HEAD_DIM, t), F32),
        jax.ShapeDtypeStruct((bp, 8, t), F32),
        jax.ShapeDtypeStruct((bp, 8, t), F32),
        jax.ShapeDtypeStruct((bp, MOBA_KV_HEADS, HEAD_DIM, 128), F32),
    ]
    out_specs = [
        headmajor(MLA_HEADS, MLA_QK), tok(MLA_QK), tok(MLA_KV_LORA), trans2(MLA_ROPE),
        headmajor(FOX_HEADS, FOX_AUG), headmajor(MOBA_HEADS, MOBA_AUG),
        trans(FOX_KV_HEADS, HEAD_DIM), trans(FOX_KV_HEADS, HEAD_DIM),
        trans(MOBA_KV_HEADS, HEAD_DIM), trans(MOBA_KV_HEADS, HEAD_DIM),
        trans2(8), trans2(8),
        pl.BlockSpec((1, MOBA_KV_HEADS, HEAD_DIM, 128), lambda b, i: (b, 0, 0, 0)),
    ]
    return pl.pallas_call(
        functools.partial(_proj_kernel, tq=tq, nblk=nblk),
        grid=(bp, nt), in_specs=in_specs, out_specs=out_specs, out_shape=out_shape,
        scratch_shapes=[pltpu.VMEM((8, 128), F32)],
        compiler_params=_params(("parallel", "arbitrary")), name="proj",
    )(x, lw["g_pre"], lw["w_tm"], lw["w_tr"], lw["w_mq"], lw["w_mk"], lw["b_ff"], lw["g_q"], lw["g_kv"], lw["w_uq"], lw["w_uk"],
      cos, sin, cost, sint)


def _online_update(carry, s, pv):
    m, l, acc = carry
    m_new = jnp.maximum(m, jnp.max(s, axis=-1, keepdims=True))
    alpha = jnp.exp(m - m_new)
    p = jnp.exp(s - m_new)
    l = alpha * l + jnp.sum(p, axis=-1, keepdims=True)
    acc = alpha * acc + pv(p.astype(BF16))
    return m_new, l, acc


def _mla_attn_kernel(q_ref, k_ref, o_ref, *, tq, tk):
    i = pl.program_id(1)
    nh = q_ref.shape[1]
    rows = nh * tq
    q = q_ref[0].reshape(rows, MLA_QK)
    nfull = (i * tq) // tk
    nall = ((i + 1) * tq + tk - 1) // tk

    def step(j, carry, masked):
        off = pl.multiple_of(j * tk, tk)
        k = k_ref[0, pl.ds(off, tk), :]
        s = _nt(q, k)
        if masked:
            qpos = i * tq + (lax.broadcasted_iota(jnp.int32, (rows, tk), 0) & (tq - 1))
            kpos = off + lax.broadcasted_iota(jnp.int32, (rows, tk), 1)
            s = jnp.where(kpos <= qpos, s, NEG)
        return _online_update(carry, s, lambda p: _nn(p, k[:, 0:MLA_KV_LORA]))

    init = (jnp.full((rows, 1), NEG, F32), jnp.zeros((rows, 1), F32), jnp.zeros((rows, MLA_KV_LORA), F32))
    carry = lax.fori_loop(0, nfull, functools.partial(step, masked=False), init)
    _, l, acc = lax.fori_loop(nfull, nall, functools.partial(step, masked=True), carry)
    o = (acc / l).astype(o_ref.dtype)
    for hh in range(nh):
        o_ref[0, :, MLA_KV_LORA * hh:MLA_KV_LORA * (hh + 1)] = o[hh * tq:(hh + 1) * tq]


def _mla_attn(qmla, kcat, tq, tk):
    bp, nh, t, _ = qmla.shape
    return pl.pallas_call(
        functools.partial(_mla_attn_kernel, tq=tq, tk=tk),
        grid=(bp, t // tq),
        in_specs=[pl.BlockSpec((1, nh, tq, MLA_QK), lambda b, i: (b, 0, i, 0)),
                  pl.BlockSpec((1, t, MLA_QK), lambda b, i: (b, 0, 0))],
        out_specs=pl.BlockSpec((1, tq, nh * MLA_KV_LORA), lambda b, i: (b, i, 0)),
        out_shape=jax.ShapeDtypeStruct((bp, t, nh * MLA_KV_LORA), BF16),
        compiler_params=_params(("parallel", "arbitrary")), name="mla_attn",
    )(qmla, kcat)


def _causal_flash(q, k_tile, v_tile, i, tq, tk):
    jd = (i * tq) // tk

    def step(j, carry, masked):
        s = _nn(q, k_tile(j))
        if masked:
            qpos = i * tq + (lax.broadcasted_iota(jnp.int32, s.shape, 0) & (tq - 1))
            kpos = j * tk + lax.broadcasted_iota(jnp.int32, s.shape, 1)
            s = jnp.where(kpos <= qpos, s, NEG)
        return _online_update(carry, s, lambda p: _nt(p, v_tile(j)))

    rows = q.shape[0]
    init = (jnp.full((rows, 1), NEG, F32), jnp.zeros((rows, 1), F32), jnp.zeros((rows, HEAD_DIM), F32))
    carry = step(jd, init, True)
    _, l, acc = lax.fori_loop(0, jd, functools.partial(step, masked=False), carry)
    return acc / l


def _fox_attn_kernel(q_ref, kt_ref, vt_ref, cumt_ref, o_ref, ka_ref, vb_ref, *, tq, tk):
    i = pl.program_id(1)
    grp = FOX_HEADS // FOX_KV_HEADS
    seq = ka_ref.shape[2]

    @pl.when(i == 0)
    def _():
        vb_ref[...] = vt_ref[0].astype(BF16)
        row = lax.broadcasted_iota(jnp.int32, (FOX_AUG - HEAD_DIM, seq), 0)
        for g in range(FOX_KV_HEADS):
            ka_ref[g, 0:HEAD_DIM, :] = kt_ref[0, g].astype(BF16)
            extra = jnp.where((row >= 6) & (row < 9), 1.0, 0.0)
            for e in range(grp):
                parts = _split3(cumt_ref[0, grp * g + e:grp * g + e + 1, :])
                for n, c in enumerate(parts):
                    extra = jnp.where(row == 3 * e + n, -c.astype(F32), extra)
            ka_ref[g, HEAD_DIM:FOX_AUG, :] = extra.astype(BF16)

    for g in range(FOX_KV_HEADS):
        q = q_ref[0, grp * g:grp * (g + 1)].reshape(grp * tq, FOX_AUG)
        k_tile = lambda j, g=g: ka_ref[g, :, pl.ds(pl.multiple_of(j * tk, tk), tk)]
        v_tile = lambda j, g=g: vb_ref[g, :, pl.ds(pl.multiple_of(j * tk, tk), tk)]
        o = _causal_flash(q, k_tile, v_tile, i, tq, tk)
        for e in range(grp):
            hh = grp * g + e
            o_ref[0, :, HEAD_DIM * hh:HEAD_DIM * (hh + 1)] = o[e * tq:(e + 1) * tq]


def _fox_attn(fq, fkt, fvt, cumt, tq, tk):
    bp, nh, t, _ = fq.shape
    assert tk % tq == 0
    kv_spec = pl.BlockSpec((1, FOX_KV_HEADS, HEAD_DIM, t), lambda b, i: (b, 0, 0, 0))
    return pl.pallas_call(
        functools.partial(_fox_attn_kernel, tq=tq, tk=tk),
        grid=(bp, t // tq),
        in_specs=[pl.BlockSpec((1, nh, tq, FOX_AUG), lambda b, i: (b, 0, i, 0)), kv_spec, kv_spec,
                  pl.BlockSpec((1, 8, t), lambda b, i: (b, 0, 0))],
        out_specs=pl.BlockSpec((1, tq, nh * HEAD_DIM), lambda b, i: (b, i, 0)),
        out_shape=jax.ShapeDtypeStruct((bp, t, nh * HEAD_DIM), F32),
        scratch_shapes=[pltpu.VMEM((FOX_KV_HEADS, FOX_AUG, t), BF16), pltpu.VMEM((FOX_KV_HEADS, HEAD_DIM, t), BF16)],
        compiler_params=_params(("parallel", "arbitrary")), name="fox_attn",
    )(fq, fkt, fvt, cumt)


def _alibi_slope(hh, n):
    return float(2.0 ** (-8.0 * (hh + 1) / n))


def _top_blocks(gate, eligible, lane):
    g = jnp.where(eligible, gate, NEG)
    sel = jnp.zeros(g.shape, jnp.bool_)
    big = g.shape[-1]
    for _ in range(MOBA_TOPK):
        mx = jnp.max(g, axis=-1, keepdims=True)
        cand = (g == mx) & (g > 0.5 * NEG)
        idx = jnp.min(jnp.where(cand, lane, big), axis=-1, keepdims=True)
        pick = lane == idx
        sel = sel | pick
        g = jnp.where(pick, NEG, g)
    return sel


_MOBA_FILL = 256


def _moba_attn_kernel(q_ref, kt_ref, vt_ref, kmt_ref, o_ref, ka_ref, vb_ref, *, tq, tk):
    i = pl.program_id(1)
    grp = MOBA_HEADS // MOBA_KV_HEADS
    seq = ka_ref.shape[2]
    nfeat = MOBA_AUG - HEAD_DIM

    @pl.when(i == 0)
    def _():
        vb_ref[...] = vt_ref[0].astype(BF16)

        def fill(c, _):
            off = pl.multiple_of(c * _MOBA_FILL, _MOBA_FILL)
            kpos = off + lax.broadcasted_iota(jnp.int32, (nfeat, _MOBA_FILL), 1)
            row = lax.broadcasted_iota(jnp.int32, (nfeat, _MOBA_FILL), 0)
            onehot = (kpos // MOBA_BLOCK) == (row - 4)
            feat = jnp.where(row == 0, kpos // 64, jnp.where(row == 1, kpos % 64, jnp.where((row < 4) | onehot, 1, 0)))
            feat = feat.astype(F32).astype(BF16)
            for g in range(MOBA_KV_HEADS):
                ka_ref[g, 0:HEAD_DIM, pl.ds(off, _MOBA_FILL)] = kt_ref[0, g, :, pl.ds(off, _MOBA_FILL)].astype(BF16)
                ka_ref[g, HEAD_DIM:MOBA_AUG, pl.ds(off, _MOBA_FILL)] = feat
            return 0

        lax.fori_loop(0, seq // _MOBA_FILL, fill, 0)

    lane = lax.broadcasted_iota(jnp.int32, (tq, MOBA_AUG), 1)
    qpos = i * tq + lax.broadcasted_iota(jnp.int32, (tq, MOBA_AUG), 0)
    def extended_query(hh):
        g = hh // grp
        slope = _alibi_slope(hh, MOBA_HEADS)
        q32 = q_ref[0, hh]
        gate = _nn_f32(q32[:, 0:HEAD_DIM], kmt_ref[0, g])
        chosen = _top_blocks(gate, lane < i, lane) | (lane == i)
        penalty = pltpu.roll(jnp.where(chosen, 0.0, NEG), HEAD_DIM + 4, axis=1)
        alibi = jnp.where(lane == HEAD_DIM, 64.0 * slope,
                          jnp.where(lane == HEAD_DIM + 1, slope,
                                    jnp.where(lane == HEAD_DIM + 2, (-64.0 * slope) * (qpos // 64).astype(F32),
                                              -slope * (qpos % 64).astype(F32))))
        return jnp.where(lane < HEAD_DIM, q32, jnp.where(lane < HEAD_DIM + 4, alibi, penalty)).astype(BF16)

    for g in range(MOBA_KV_HEADS):
        q_aug = jnp.concatenate([extended_query(grp * g + e) for e in range(grp)], axis=0)
        k_tile = lambda j, g=g: ka_ref[g, :, pl.ds(pl.multiple_of(j * tk, tk), tk)]
        v_tile = lambda j, g=g: vb_ref[g, :, pl.ds(pl.multiple_of(j * tk, tk), tk)]
        o = _causal_flash(q_aug, k_tile, v_tile, i, tq, tk)
        for e in range(grp):
            hh = grp * g + e
            o_ref[0, :, HEAD_DIM * hh:HEAD_DIM * (hh + 1)] = o[e * tq:(e + 1) * tq]


def _moba_attn(mq, mkt, mvt, kmt, tk):
    bp, nh, t, _ = mq.shape
    tq = MOBA_BLOCK
    assert tk % tq == 0 and t % _MOBA_FILL == 0 and t // MOBA_BLOCK <= MOBA_MAX_BLOCKS
    kv_spec = pl.BlockSpec((1, MOBA_KV_HEADS, HEAD_DIM, t), lambda b, i: (b, 0, 0, 0))
    return pl.pallas_call(
        functools.partial(_moba_attn_kernel, tq=tq, tk=tk),
        grid=(bp, t // tq),
        in_specs=[pl.BlockSpec((1, nh, tq, MOBA_AUG), lambda b, i: (b, 0, i, 0)), kv_spec, kv_spec,
                  pl.BlockSpec((1, MOBA_KV_HEADS, HEAD_DIM, 128), lambda b, i: (b, 0, 0, 0))],
        out_specs=pl.BlockSpec((1, tq, nh * HEAD_DIM), lambda b, i: (b, i, 0)),
        out_shape=jax.ShapeDtypeStruct((bp, t, nh * HEAD_DIM), F32),
        scratch_shapes=[pltpu.VMEM((MOBA_KV_HEADS, MOBA_AUG, t), BF16), pltpu.VMEM((MOBA_KV_HEADS, HEAD_DIM, t), BF16)],
        compiler_params=_params(("parallel", "arbitrary")), name="moba_attn",
    )(mq, mkt, mvt, kmt)


class _PagedStream:
    def __init__(self, pt_ref, copies, n_pages, pp, reverse):
        self.pt_ref, self.copies, self.pp = pt_ref, copies, pp
        self.nc = n_pages // pp
        self.n_pages = n_pages
        self.reverse = reverse

    def chunk_of(self, step):
        seq = step // self.nc
        c = step % self.nc
        if self.reverse:
            c = self.nc - 1 - c
        return seq * self.n_pages + c * self.pp, c

    def _descs(self, step, slot):
        base, _ = self.chunk_of(step)
        out = []
        for pg in range(self.pp):
            out.extend(self.copies(self.pt_ref[base + pg], slot, pg))
        return out

    def start(self, step, slot):
        for d in self._descs(step, slot):
            d.start()

    def wait(self, step, slot):
        for d in self._descs(step, slot):
            d.wait()

    def run(self, body, init):
        b = pl.program_id(0)
        total = pl.num_programs(0) * self.nc

        @pl.when(b == 0)
        def _():
            self.start(0, 0)

        def loop(cc, carry):
            step = b * self.nc + cc
            slot = step % 2

            @pl.when(step + 1 < total)
            def _():
                self.start(step + 1, 1 - slot)

            self.wait(step, slot)
            _, c = self.chunk_of(step)
            return body(c, slot, carry)

        return lax.fori_loop(0, self.nc, loop, init)


def _mla_dec_kernel(pt_ref, q_ref, knew_ref, ckv_hbm, kpet_hbm, o_ref, ckv_buf, kpet_buf, sem,
                    *, layer, n_pages, pp):
    n = pp * PAGE_SIZE

    def copies(page, slot, pg):
        return [
            pltpu.make_async_copy(ckv_hbm.at[layer, page],
                                  ckv_buf.at[slot, pl.ds(pg * PAGE_SIZE, PAGE_SIZE)], sem.at[0, slot]),
            pltpu.make_async_copy(kpet_hbm.at[layer, page],
                                  kpet_buf.at[slot, :, pl.ds(pg * PAGE_SIZE, PAGE_SIZE)], sem.at[1, slot]),
        ]

    stream = _PagedStream(pt_ref, copies, n_pages, pp, reverse=False)
    q = q_ref[0]
    knew = knew_ref[0]
    s_new = jnp.sum(q.astype(F32) * knew.astype(F32), axis=-1, keepdims=True)
    v_new = jnp.broadcast_to(knew[:, 0:MLA_KV_LORA].astype(F32), (MLA_HEADS, MLA_KV_LORA))

    def body(c, slot, carry):
        kv = ckv_buf[slot].astype(BF16)
        kt = kpet_buf[slot].astype(BF16)
        s = _nt(q[:, 0:MLA_KV_LORA], kv) + _nn(q[:, MLA_KV_LORA:MLA_QK], kt)
        return _online_update(carry, s, lambda p: _nn(p, kv))

    init = (s_new, jnp.ones((MLA_HEADS, 1), F32), v_new)
    _, l, acc = stream.run(body, init)
    o_ref[0] = (acc / l).astype(o_ref.dtype)


def _mla_dec(pt, q, knew, cache_ckv, cache_kpet, layer, pp):
    db = q.shape[0]
    n_pages = pt.shape[0] // db
    n = pp * PAGE_SIZE
    grid_spec = pltpu.PrefetchScalarGridSpec(
        num_scalar_prefetch=1, grid=(db,),
        in_specs=[pl.BlockSpec((1, MLA_HEADS, MLA_QK), lambda b, pt: (b, 0, 0)),
                  pl.BlockSpec((1, 1, MLA_QK), lambda b, pt: (b, 0, 0)),
                  pl.BlockSpec(memory_space=pl.ANY), pl.BlockSpec(memory_space=pl.ANY)],
        out_specs=pl.BlockSpec((1, MLA_HEADS, MLA_KV_LORA), lambda b, pt: (b, 0, 0)),
        scratch_shapes=[pltpu.VMEM((2, n, MLA_KV_LORA), F32), pltpu.VMEM((2, MLA_ROPE, n), F32),
                        pltpu.SemaphoreType.DMA((2, 2))])
    return pl.pallas_call(
        functools.partial(_mla_dec_kernel, layer=layer, n_pages=n_pages, pp=pp),
        grid_spec=grid_spec, out_shape=jax.ShapeDtypeStruct((db, MLA_HEADS, MLA_KV_LORA), BF16),
        compiler_params=_params(("arbitrary",)), name="mla_dec",
    )(pt, q, knew, cache_ckv, cache_kpet)


def _rows_by_group(parts, grp):
    out = parts[-1]
    rowid = lax.broadcasted_iota(jnp.int32, out.shape, 0)
    for g in range(len(parts) - 2, -1, -1):
        out = jnp.where(rowid < (g + 1) * grp, parts[g], out)
    return out


def _fox_dec_kernel(pt_ref, q_ref, knew_ref, vnew_ref, lfnew_ref, kt_hbm, vt_hbm, lft_hbm, o_ref,
                    kt_buf, vt_buf, lf_buf, sem, *, layer, n_pages, pp):
    b = pl.program_id(0)
    grp = FOX_HEADS // FOX_KV_HEADS

    def copies(page, slot, pg):
        return [
            pltpu.make_async_copy(kt_hbm.at[layer, page],
                                  kt_buf.at[slot, :, :, pl.ds(pg * PAGE_SIZE, PAGE_SIZE)], sem.at[0, slot]),
            pltpu.make_async_copy(vt_hbm.at[layer, page],
                                  vt_buf.at[slot, :, :, pl.ds(pg * PAGE_SIZE, PAGE_SIZE)], sem.at[1, slot]),
            pltpu.make_async_copy(lft_hbm.at[layer, page],
                                  lf_buf.at[slot, pg, 0:FOX_HEADS], sem.at[2, slot]),
        ]

    @pl.when(b == 0)
    def _():
        lf_buf[...] = jnp.zeros_like(lf_buf)

    stream = _PagedStream(pt_ref, copies, n_pages, pp, reverse=True)
    q = q_ref[0]
    s_new = jnp.sum(q.astype(F32) * knew_ref[0], axis=-1, keepdims=True)
    row = lax.broadcasted_iota(jnp.int32, (PAGE_SIZE, PAGE_SIZE), 0)
    col = lax.broadcasted_iota(jnp.int32, (PAGE_SIZE, PAGE_SIZE), 1)
    later = (row > col).astype(BF16)

    def body(c, slot, carry):
        m, l, acc, run = carry
        lf = lf_buf[slot].reshape(pp * 8, PAGE_SIZE)
        within = _dot3(lf, later)
        tot = jnp.sum(lf, axis=-1, keepdims=True)
        bias = [None] * pp
        for pg in range(pp - 1, -1, -1):
            bias[pg] = within[8 * pg:8 * (pg + 1)] + run
            run = run + tot[8 * pg:8 * (pg + 1)]
        bias = jnp.concatenate(bias, axis=-1)
        kt = kt_buf[slot].astype(BF16)
        vt = vt_buf[slot].astype(BF16)
        s = _rows_by_group([_nn(q, kt[g]) for g in range(FOX_KV_HEADS)], grp) + bias
        pv = lambda p: _rows_by_group([_nt(p, vt[g]) for g in range(FOX_KV_HEADS)], grp)
        m, l, acc = _online_update((m, l, acc), s, pv)
        return m, l, acc, run

    init = (s_new, jnp.ones((8, 1), F32), vnew_ref[0], lfnew_ref[0])
    _, l, acc, _ = stream.run(body, init)
    o_ref[0] = acc / l


def _kv_dec_specs(db, with_lf):
    specs = [pl.BlockSpec((1, 8, HEAD_DIM), lambda b, pt: (b, 0, 0))] * 3
    if with_lf:
        specs.append(pl.BlockSpec((1, 8, 1), lambda b, pt: (b, 0, 0)))
    return specs


def _fox_dec(pt, q, knew, vnew, lfnew, cache_kt, cache_vt, cache_lft, layer, pp):
    db = q.shape[0]
    n_pages = pt.shape[0] // db
    n = pp * PAGE_SIZE
    grid_spec = pltpu.PrefetchScalarGridSpec(
        num_scalar_prefetch=1, grid=(db,),
        in_specs=_kv_dec_specs(db, True) + [pl.BlockSpec(memory_space=pl.ANY)] * 3,
        out_specs=pl.BlockSpec((1, 8, HEAD_DIM), lambda b, pt: (b, 0, 0)),
        scratch_shapes=[pltpu.VMEM((2, FOX_KV_HEADS, HEAD_DIM, n), F32),
                        pltpu.VMEM((2, FOX_KV_HEADS, HEAD_DIM, n), F32),
                        pltpu.VMEM((2, pp, 8, PAGE_SIZE), F32),
                        pltpu.SemaphoreType.DMA((3, 2))])
    return pl.pallas_call(
        functools.partial(_fox_dec_kernel, layer=layer, n_pages=n_pages, pp=pp),
        grid_spec=grid_spec, out_shape=jax.ShapeDtypeStruct((db, 8, HEAD_DIM), F32),
        compiler_params=_params(("arbitrary",)), name="fox_dec",
    )(pt, q, knew, vnew, lfnew, cache_kt, cache_vt, cache_lft)


def _moba_dec_kernel(pt_ref, q_ref, knew_ref, vnew_ref, kt_hbm, vt_hbm, o_ref,
                     kt_buf, vt_buf, ksum_ref, m_ref, l_ref, acc_ref, sem, *, layer, n_pages, pp):
    grp = MOBA_HEADS // MOBA_KV_HEADS
    n = pp * PAGE_SIZE
    bpc = n // MOBA_BLOCK
    nblocks = n_pages * PAGE_SIZE // MOBA_BLOCK
    past = n_pages * PAGE_SIZE

    def copies(page, slot, pg):
        return [
            pltpu.make_async_copy(kt_hbm.at[layer, page],
                                  kt_buf.at[slot, :, :, pl.ds(pg * PAGE_SIZE, PAGE_SIZE)], sem.at[0, slot]),
            pltpu.make_async_copy(vt_hbm.at[layer, page],
                                  vt_buf.at[slot, :, :, pl.ds(pg * PAGE_SIZE, PAGE_SIZE)], sem.at[1, slot]),
        ]

    stream = _PagedStream(pt_ref, copies, n_pages, pp, reverse=False)
    q32 = q_ref[0]
    q = q32.astype(BF16)
    rowid = lax.broadcasted_iota(jnp.int32, (8, 1), 0)
    slopes = jnp.zeros((8, 1), F32)
    for hh in range(MOBA_HEADS):
        slopes = jnp.where(rowid == hh, _alibi_slope(hh, MOBA_HEADS), slopes)
    lanepos = lax.broadcasted_iota(jnp.int32, (8, n), 1)
    blklane = lax.broadcasted_iota(jnp.int32, (HEAD_DIM, 128), 1)
    ksum_ref[...] = jnp.zeros_like(ksum_ref)

    def body(c, slot, carry):
        kt = kt_buf[slot].astype(BF16)
        vt = vt_buf[slot].astype(BF16)
        raw = _rows_by_group([_nn(q, kt[g]) for g in range(MOBA_KV_HEADS)], grp)
        dist = (past - c * n - lanepos).astype(F32)
        s = raw - slopes * dist
        for jj in range(bpc):
            lo, hi = jj * MOBA_BLOCK, (jj + 1) * MOBA_BLOCK
            blk = c * bpc + jj
            for g in range(MOBA_KV_HEADS):
                ksum = jnp.sum(kt_buf[slot, g, :, lo:hi], axis=-1, keepdims=True)
                ksum_ref[g] = jnp.where(blklane == blk, ksum, ksum_ref[g])
            sb = s[:, lo:hi]
            m = jnp.max(sb, axis=-1, keepdims=True)
            p = jnp.exp(sb - m)
            l = jnp.sum(p, axis=-1, keepdims=True)
            pb = p.astype(BF16)
            acc = _rows_by_group([_nt(pb, vt[g][:, lo:hi]) for g in range(MOBA_KV_HEADS)], grp)
            m_ref[blk] = jnp.broadcast_to(m, (8, 128))
            l_ref[blk] = jnp.broadcast_to(l, (8, 128))
            acc_ref[blk] = acc
        return carry

    stream.run(body, 0)

    gate = _rows_by_group([_nn_f32(q32, ksum_ref[g]) for g in range(MOBA_KV_HEADS)], grp)
    g = jnp.stack([jnp.broadcast_to(gate[:, bb:bb + 1], (8, 128)) for bb in range(nblocks)])
    blkid = lax.broadcasted_iota(jnp.int32, g.shape, 0)
    sel = jnp.zeros(g.shape, jnp.bool_)
    for _ in range(min(MOBA_TOPK, nblocks)):
        mx = jnp.max(g, axis=0, keepdims=True)
        cand = (g == mx) & (g > 0.5 * NEG)
        idx = jnp.min(jnp.where(cand, blkid, nblocks), axis=0, keepdims=True)
        pick = blkid == idx
        sel = sel | pick
        g = jnp.where(pick, NEG, g)
    s_new = jnp.sum(q.astype(F32) * knew_ref[0], axis=-1, keepdims=True)
    mb = m_ref[...]
    mtop = jnp.maximum(jnp.max(jnp.where(sel, mb, NEG), axis=0), s_new)
    w = jnp.where(sel, jnp.exp(mb - mtop[None]), 0.0)
    w_new = jnp.exp(s_new - mtop)
    denom = jnp.sum(w * l_ref[...], axis=0) + w_new
    num = jnp.sum(w[:, :, 0:HEAD_DIM] * acc_ref[...], axis=0) + w_new[:, 0:HEAD_DIM] * vnew_ref[0]
    o_ref[0] = num / denom[:, 0:HEAD_DIM]


def _moba_dec(pt, q, knew, vnew, cache_kt, cache_vt, layer, pp):
    db = q.shape[0]
    n_pages = pt.shape[0] // db
    n = pp * PAGE_SIZE
    nblocks = n_pages * PAGE_SIZE // MOBA_BLOCK
    assert nblocks <= 128
    grid_spec = pltpu.PrefetchScalarGridSpec(
        num_scalar_prefetch=1, grid=(db,),
        in_specs=_kv_dec_specs(db, False) + [pl.BlockSpec(memory_space=pl.ANY)] * 2,
        out_specs=pl.BlockSpec((1, 8, HEAD_DIM), lambda b, pt: (b, 0, 0)),
        scratch_shapes=[pltpu.VMEM((2, MOBA_KV_HEADS, HEAD_DIM, n), F32),
                        pltpu.VMEM((2, MOBA_KV_HEADS, HEAD_DIM, n), F32),
                        pltpu.VMEM((MOBA_KV_HEADS, HEAD_DIM, 128), F32), pltpu.VMEM((nblocks, 8, 128), F32),
                        pltpu.VMEM((nblocks, 8, 128), F32), pltpu.VMEM((nblocks, 8, HEAD_DIM), F32),
                        pltpu.SemaphoreType.DMA((2, 2))])
    return pl.pallas_call(
        functools.partial(_moba_dec_kernel, layer=layer, n_pages=n_pages, pp=pp),
        grid_spec=grid_spec, out_shape=jax.ShapeDtypeStruct((db, 8, HEAD_DIM), F32),
        compiler_params=_params(("arbitrary",)), name="moba_dec",
    )(pt, q, knew, vnew, cache_kt, cache_vt)


def _mix_kernel(olat_ref, ofox_ref, omoba_ref, x_ref, wuv_ref, gn_ref, wout_ref, gpost_ref, gffn_ref,
                x1_ref, h2_ref):
    n_mla = MLA_HEADS * MLA_V
    n_fox = FOX_HEADS * HEAD_DIM
    gn = gn_ref[...]
    o_mla = _nn(olat_ref[...], wuv_ref[...])
    o = jnp.concatenate([_rms(o_mla, gn[:, 0:n_mla]),
                         _rms(ofox_ref[...], gn[:, n_mla:n_mla + n_fox]),
                         _rms(omoba_ref[...], gn[:, n_mla + n_fox:])], axis=-1).astype(BF16)
    x1 = x_ref[...] + _rms(_nn(o, wout_ref[...]), gpost_ref[...])
    x1_ref[...] = x1
    h2_ref[...] = _rms(x1, gffn_ref[...]).astype(BF16)


def _mix(olat, ofox, omoba, x, lw, tq):
    t, d = x.shape
    row = lambda w: pl.BlockSpec((tq, w), lambda i: (i, 0))
    const = lambda *shape: pl.BlockSpec(shape, lambda i: (0,) * len(shape))
    return pl.pallas_call(
        _mix_kernel, grid=(t // tq,),
        in_specs=[row(olat.shape[1]), row(ofox.shape[1]), row(omoba.shape[1]), row(d),
                  const(*lw["w_uv"].shape), const(1, d), const(*lw["w_out"].shape), const(1, d), const(1, d)],
        out_specs=[row(d), row(d)],
        out_shape=[jax.ShapeDtypeStruct((t, d), F32), jax.ShapeDtypeStruct((t, d), BF16)],
        compiler_params=_params(("parallel",)), name="mix_out",
    )(olat, ofox, omoba, x, lw["w_uv"], lw["g_group"], lw["w_out"], lw["g_post"], lw["g_ffn_pre"])


def _ffn_kernel(x1_ref, h2_ref, ple_ref, wg_ref, wu_ref, wd_ref, gpost_ref, wpp_ref, wpg_ref, y_ref, acc_ref):
    j = pl.program_id(1)

    @pl.when(j == 0)
    def _():
        acc_ref[...] = jnp.zeros_like(acc_ref)

    h2 = h2_ref[...]
    gate = _nn(h2, wg_ref[...])
    act = (gate * jax.nn.sigmoid(gate)) * _nn(h2, wu_ref[...])
    acc_ref[...] += _nn(act.astype(BF16), wd_ref[...])

    @pl.when(j == pl.num_programs(1) - 1)
    def _():
        x2 = x1_ref[...] + _rms(acc_ref[...], gpost_ref[...])
        emb = _nn(ple_ref[...].astype(BF16), wpp_ref[...])
        y_ref[...] = x2 + emb * jax.nn.sigmoid(_nn(x2.astype(BF16), wpg_ref[...]))


def _ffn(x1, h2, ple, lw, tq, th):
    t, d = x1.shape
    hid = lw["w_gate"].shape[1]
    row = lambda w: pl.BlockSpec((tq, w), lambda i, j: (i, 0))
    const = lambda *shape: pl.BlockSpec(shape, lambda i, j: (0,) * len(shape))
    return pl.pallas_call(
        _ffn_kernel, grid=(t // tq, hid // th),
        in_specs=[row(d), row(d), row(ple.shape[1]),
                  pl.BlockSpec((d, th), lambda i, j: (0, j)), pl.BlockSpec((d, th), lambda i, j: (0, j)),
                  pl.BlockSpec((th, d), lambda i, j: (j, 0)),
                  const(1, d), const(*lw["w_ple_proj"].shape), const(*lw["w_ple_gate"].shape)],
        out_specs=row(d), out_shape=jax.ShapeDtypeStruct((t, d), F32),
        scratch_shapes=[pltpu.VMEM((tq, d), F32)],
        compiler_params=_params(("parallel", "arbitrary")), name="ffn",
    )(x1, h2, ple, lw["w_gate"], lw["w_up"], lw["w_down"], lw["g_ffn_post"], lw["w_ple_proj"], lw["w_ple_gate"])


def _rope_tables(pos):
    inv = 1.0 / (ROPE_THETA ** (np.arange(0, MLA_ROPE, 2, dtype=np.float32) / MLA_ROPE))
    ang = pos.astype(F32)[:, None] * jnp.asarray(inv, F32)
    c, s = jnp.cos(ang), jnp.sin(ang)
    cos32 = jnp.concatenate([c, c], axis=-1)
    sin32 = jnp.concatenate([-s, s], axis=-1)
    return (jnp.tile(cos32, (1, MLA_HEADS)), jnp.tile(sin32, (1, MLA_HEADS)),
            jnp.transpose(cos32), jnp.transpose(sin32))


def _layer_weights(i, w):
    d = w["w_in"].shape[1]
    wt = jnp.transpose(w["w_in"], (2, 0, 1))[:, i, :]
    o = np.cumsum([0, MLA_Q_LORA, MLA_KV_LORA, MLA_ROPE, FOX_HEADS * HEAD_DIM, FOX_KV_HEADS * HEAD_DIM,
                   FOX_KV_HEADS * HEAD_DIM, FOX_HEADS, MOBA_HEADS * HEAD_DIM, MOBA_KV_HEADS * HEAD_DIM,
                   MOBA_KV_HEADS * HEAD_DIM]).tolist()
    cq, ckv, kpe, fq, fk, fv, ff, mq, mk, mv = [wt[o[k]:o[k + 1]] for k in range(10)]
    half = MLA_ROPE // 2
    swap = np.concatenate([np.arange(half, MLA_ROPE), np.arange(0, half)])
    kpe_sw = kpe[swap]
    w_tm = jnp.concatenate([cq, ckv, fq, kpe, kpe_sw], axis=0).astype(BF16)
    w_tr = jnp.concatenate([fk, fv, mv, kpe, kpe_sw, ff, jnp.zeros((8 - FOX_HEADS, d), F32)], axis=0).astype(BF16)
    b_ff = jnp.concatenate([w["b_fox_f"][i], jnp.zeros((8 - FOX_HEADS,), F32)])[:, None]

    per = MLA_NOPE + MLA_ROPE
    heads = np.arange(MLA_HEADS)[:, None] * per
    nope_cols = (heads + np.arange(MLA_NOPE)[None]).reshape(-1)
    pe_cols = (heads + MLA_NOPE + np.arange(MLA_ROPE)[None]).reshape(-1)
    pesw_cols = (heads + MLA_NOPE + swap[None]).reshape(-1)
    w_uq = w["w_mla_uq"][i][:, np.concatenate([nope_cols, pe_cols, pesw_cols])].astype(BF16)

    uk = jnp.transpose(w["w_mla_uk"][i], (1, 2, 0))
    z = jnp.zeros((MLA_NOPE, MLA_KV_LORA), F32)
    w_uk = jnp.stack([jnp.concatenate([jnp.concatenate([uk[2 * j], z], axis=1),
                                       jnp.concatenate([z, uk[2 * j + 1]], axis=1)], axis=0)
                      for j in range(MLA_HEADS // 2)]).astype(BF16)
    uv = w["w_mla_uv"][i]
    w_uv = jnp.zeros((MLA_HEADS * MLA_KV_LORA, MLA_HEADS * MLA_V), F32)
    for hh in range(MLA_HEADS):
        w_uv = lax.dynamic_update_slice(w_uv, uv[:, hh, :], (hh * MLA_KV_LORA, hh * MLA_V))
    row = lambda a: a[i][None, :]
    return dict(
        g_pre=row(w["norm_mix_pre"]), w_tm=w_tm, w_tr=w_tr,
        w_mq=jnp.concatenate([mq, jnp.zeros((HEAD_DIM, d), F32)], axis=0), w_mk=mk, b_ff=b_ff, g_q=row(w["mla_q_norm"]),
        g_kv=row(w["mla_kv_norm"]), w_uq=w_uq, w_uk=w_uk, w_uv=w_uv.astype(BF16),
        g_group=row(w["group_norm"]), w_out=w["w_out"][i].astype(BF16), g_post=row(w["norm_mix_post"]),
        g_ffn_pre=row(w["norm_ffn_pre"]), g_ffn_post=row(w["norm_ffn_post"]),
        w_gate=w["w_ffn_gate"][i].astype(BF16), w_up=w["w_ffn_up"][i].astype(BF16),
        w_down=w["w_ffn_down"][i].astype(BF16), w_ple_proj=w["w_ple_proj"][i].astype(BF16),
        w_ple_gate=w["w_ple_gate"][i].astype(BF16))


def _pick(n, prefs):
    for p in prefs:
        if n % p == 0:
            return p
    return n


def _cache_rows(m):
    kv = lambda a: jnp.transpose(a, (0, 3, 1, 2))
    return (m["ckv"], jnp.transpose(m["kpet"], (0, 2, 1)), kv(m["fkt"]), kv(m["fvt"]),
            jnp.transpose(m["logft"][:, 0:FOX_HEADS], (0, 2, 1)), kv(m["mkt"]), kv(m["mvt"]))


_PROJ_NAMES = ("qmla", "kcat", "ckv", "kpet", "fq", "mq", "fkt", "fvt", "mkt", "mvt", "logft", "cumt", "kmt")


def _transposed_caches(caches):
    ckv, kpe, fk, fv, lf, mk, mv = caches
    kv = lambda a: jnp.transpose(a, (0, 1, 3, 4, 2))
    return (ckv, jnp.transpose(kpe, (0, 1, 3, 2)), kv(fk), kv(fv), jnp.transpose(lf, (0, 1, 3, 2)), kv(mk), kv(mv))


def _sample_attention(ys, lw, layer, page_table, caches_t):
    cache_ckv, cache_kpet, cache_fkt, cache_fvt, cache_lft, cache_mkt, cache_mvt = caches_t
    db = ys.shape[1]
    n_pages = page_table.shape[1]
    pt_flat = page_table.reshape(-1)
    pp = _pick(n_pages, (32, 16, 8, 4, 2))
    tables_s = _rope_tables(jnp.full((db,), n_pages * PAGE_SIZE, jnp.int32))
    pad_heads = lambda a: jnp.concatenate([a, jnp.zeros((db, 8 - a.shape[1]) + a.shape[2:], a.dtype)], axis=1)
    grp_rows = np.array([0, 0, 1, 1, 0, 0, 0, 0])
    tokmajor = lambda a: jnp.transpose(a[0], (2, 0, 1))

    ms = dict(zip(_PROJ_NAMES, _project(ys, lw, tables_s, db)))
    q_mla = jnp.transpose(ms["qmla"][0], (1, 0, 2))
    knew = ms["kcat"][0][:, None, :]
    o_lat = _mla_dec(pt_flat, q_mla, knew, cache_ckv, cache_kpet, layer, pp)
    fq = pad_heads(jnp.transpose(ms["fq"][0, :, :, 0:HEAD_DIM], (1, 0, 2)))
    lf_new = jnp.transpose(ms["logft"][0])[:, :, None]
    o_fox = _fox_dec(pt_flat, fq, tokmajor(ms["fkt"])[:, grp_rows], tokmajor(ms["fvt"])[:, grp_rows],
                     lf_new, cache_fkt, cache_fvt, cache_lft, layer, pp)
    mq = pad_heads(jnp.transpose(ms["mq"][0, :, :, 0:HEAD_DIM], (1, 0, 2)))
    o_moba = _moba_dec(pt_flat, mq, tokmajor(ms["mkt"])[:, grp_rows], tokmajor(ms["mvt"])[:, grp_rows],
                       cache_mkt, cache_mvt, layer, pp)
    return ms, o_lat, o_fox, o_moba


def kernel(x_prompt, x_sample, cache_mla_ckv, cache_mla_kpe, cache_fox_k, cache_fox_v, cache_fox_logf,
           cache_moba_k, cache_moba_v, page_table, p_prompt, p_sample, norm_mix_pre, norm_mix_post,
           norm_ffn_pre, norm_ffn_post, w_in, b_fox_f, mla_q_norm, mla_kv_norm, w_mla_uq, w_mla_uk,
           w_mla_uv, group_norm, w_out, w_ffn_gate, w_ffn_up, w_ffn_down, w_ple_proj, w_ple_gate):
    w = dict(norm_mix_pre=norm_mix_pre, norm_mix_post=norm_mix_post, norm_ffn_pre=norm_ffn_pre,
             norm_ffn_post=norm_ffn_post, w_in=w_in, b_fox_f=b_fox_f, mla_q_norm=mla_q_norm,
             mla_kv_norm=mla_kv_norm, w_mla_uq=w_mla_uq, w_mla_uk=w_mla_uk, w_mla_uv=w_mla_uv,
             group_norm=group_norm, w_out=w_out, w_ffn_gate=w_ffn_gate, w_ffn_up=w_ffn_up,
             w_ffn_down=w_ffn_down, w_ple_proj=w_ple_proj, w_ple_gate=w_ple_gate)
    depth = w_in.shape[0]
    bsz, seq, d = x_prompt.shape
    db = x_sample.shape[0]
    n_pages = page_table.shape[1]
    past = n_pages * PAGE_SIZE
    assert x_sample.shape[1] == 1 and seq % MOBA_BLOCK == 0 and past % MOBA_BLOCK == 0

    caches_t = _transposed_caches((cache_mla_ckv, cache_mla_kpe, cache_fox_k, cache_fox_v, cache_fox_logf,
                                   cache_moba_k, cache_moba_v))
    tables_p = _rope_tables(jnp.arange(seq))
    tq_proj = _pick(seq, (512, 256))
    tq_tail = _pick(seq, (512, 256, 128))
    th = _pick(w_ffn_gate.shape[2], (1408, 256, 128))
    tk_attn = _pick(seq, (512, 256))

    yp = x_prompt
    ys = x_sample.reshape(1, db, d)
    rows_p = [[] for _ in range(7)]
    rows_s = [[] for _ in range(7)]
    for i in range(depth):
        lw = _layer_weights(i, w)

        m = dict(zip(_PROJ_NAMES, _project(yp, lw, tables_p, tq_proj)))
        o_lat = _mla_attn(m["qmla"], m["kcat"], 128, tk_attn)
        o_fox = _fox_attn(m["fq"], m["fkt"], m["fvt"], m["cumt"], 256, tk_attn)
        o_moba = _moba_attn(m["mq"], m["mkt"], m["mvt"], m["kmt"], tk_attn)
        flat = lambda a: a.reshape(bsz * seq, a.shape[-1])
        x1, h2 = _mix(flat(o_lat), flat(o_fox), flat(o_moba), flat(yp), lw, tq_tail)
        yp = _ffn(x1, h2, flat(p_prompt[i]), lw, tq_tail, th).reshape(bsz, seq, d)
        for lst, r in zip(rows_p, _cache_rows(m)):
            lst.append(r)

        ms, o_lat_s, o_fox_s, o_moba_s = _sample_attention(ys, lw, i, page_table, caches_t)
        heads4 = lambda a: a[:, 0:4].reshape(db, 4 * HEAD_DIM)
        x1s, h2s = _mix(o_lat_s.reshape(db, MLA_HEADS * MLA_KV_LORA), heads4(o_fox_s), heads4(o_moba_s),
                        ys[0], lw, db)
        ys = _ffn(x1s, h2s, p_sample[i][:, 0, :], lw, db, th).reshape(1, db, d)
        for lst, r in zip(rows_s, _cache_rows(ms)):
            lst.append(jnp.transpose(r, (1, 0) + tuple(range(2, r.ndim))))

    outs = [yp, ys.reshape(db, 1, d)]
    for rp, rs in zip(rows_p, rows_s):
        outs.append(jnp.stack(rp))
        outs.append(jnp.stack(rs))
    return tuple(outs)
```

```python
import functools

import numpy as np
import jax
import jax.numpy as jnp
from jax import lax
from jax.experimental import pallas as pl
from jax.experimental.pallas import tpu as pltpu

HEAD_DIM = 64
MLA_HEADS = 8
MLA_Q_LORA = 256
MLA_KV_LORA = 128
MLA_NOPE = 64
MLA_ROPE = 32
MLA_V = 64
ROPE_THETA = 10000.0
FOX_HEADS = 4
FOX_KV_HEADS = 2
MOBA_HEADS = 4
MOBA_KV_HEADS = 2
MOBA_BLOCK = 256
MOBA_TOPK = 3
PAGE_SIZE = 128
NORM_EPS = 1e-6
MLA_QK = MLA_KV_LORA + MLA_ROPE
FOX_AUG = HEAD_DIM + 16
MOBA_AUG = 128
MOBA_MAX_BLOCKS = MOBA_AUG - HEAD_DIM - 4
V_AUG = HEAD_DIM + 16

BF16 = jnp.bfloat16
F32 = jnp.float32
NEG = -1e30
VMEM_LIMIT = 56 * 1024 * 1024

def _nn(a, b):
    return jnp.dot(a, b, preferred_element_type=F32)


def _nt(a, b):
    return lax.dot_general(a, b, (((1,), (1,)), ((), ())), preferred_element_type=F32)


def _nt_f32(a, b):
    return lax.dot_general(a, b, (((1,), (1,)), ((), ())), precision=lax.Precision.HIGHEST,
                           preferred_element_type=F32)


def _nn_f32(a, b):
    return jnp.dot(a, b, precision=lax.Precision.HIGHEST, preferred_element_type=F32)


def _rms(x, g):
    return x * lax.rsqrt(jnp.mean(x * x, axis=-1, keepdims=True) + NORM_EPS) * g


def _split3(x):
    hi = x.astype(BF16)
    r = x - hi.astype(F32)
    mid = r.astype(BF16)
    lo = (r - mid.astype(F32)).astype(BF16)
    return hi, mid, lo


def _dot3(x, w):
    hi, mid, lo = _split3(x)
    return (_nn(hi, w) + _nn(mid, w)) + _nn(lo, w)


def _params(sem):
    return pltpu.CompilerParams(dimension_semantics=sem, vmem_limit_bytes=VMEM_LIMIT)


_Z_CQ, _Z_CKV, _Z_FQ, _Z_KPE, _Z_KPESW, _Z_END = 0, 256, 384, 640, 672, 704
_T_FK, _T_FV, _T_MV, _T_KPE, _T_KPESW, _T_FF, _T_END = 0, 128, 256, 384, 416, 448, 456


def _proj_kernel(x_ref, gpre_ref, wtm_ref, wtr_ref, wmq_ref, wmk_ref, bff_ref, gq_ref, gkv_ref, wuq_ref, wuk_ref,
                 cos_ref, sin_ref, cost_ref, sint_ref,
                 qmla_ref, kcat_ref, ckv_ref, kpet_ref, fq_ref, mq_ref,
                 fkt_ref, fvt_ref, mkt_ref, mvt_ref, logft_ref, cumt_ref, kmt_ref,
                 carry_ref, *, tq, nblk):
    t = pl.program_id(1)
    mla_scale = (MLA_NOPE + MLA_ROPE) ** -0.5
    qk_scale = HEAD_DIM ** -0.5

    x = x_ref[0]
    hf = _rms(x, gpre_ref[...])
    h = hf.astype(BF16)
    z = _nt(h, wtm_ref[...])
    zt = _nt(wtr_ref[...], h)
    zmq = _nt_f32(hf, wmq_ref[...])
    zmk = _nt_f32(wmk_ref[...], hf)

    cqn = _rms(z[:, _Z_CQ:_Z_CKV], gq_ref[...]).astype(BF16)
    q = _nn(cqn, wuq_ref[...])
    nrope = MLA_HEADS * MLA_ROPE
    nnope = MLA_HEADS * MLA_NOPE
    pe = (q[:, nnope:nnope + nrope] * cos_ref[...]
          + q[:, nnope + nrope:nnope + 2 * nrope] * sin_ref[...]) * mla_scale
    for j in range(MLA_HEADS // 2):
        lat2 = _nn(q[:, 128 * j:128 * (j + 1)].astype(BF16), wuk_ref[j]) * mla_scale
        qmla_ref[0, 2 * j, :, 0:MLA_KV_LORA] = lat2[:, 0:MLA_KV_LORA].astype(BF16)
        qmla_ref[0, 2 * j + 1, :, 0:MLA_KV_LORA] = lat2[:, MLA_KV_LORA:].astype(BF16)
    for hh in range(MLA_HEADS):
        qmla_ref[0, hh, :, MLA_KV_LORA:MLA_QK] = pe[:, MLA_ROPE * hh:MLA_ROPE * (hh + 1)].astype(BF16)

    ckv = _rms(z[:, _Z_CKV:_Z_FQ], gkv_ref[...])
    ckv_ref[0] = ckv
    kcat_ref[0, :, 0:MLA_KV_LORA] = ckv.astype(BF16)
    kpe = z[:, _Z_KPE:_Z_KPESW] * cos_ref[:, 0:MLA_ROPE] + z[:, _Z_KPESW:_Z_END] * sin_ref[:, 0:MLA_ROPE]
    kcat_ref[0, :, MLA_KV_LORA:MLA_QK] = kpe.astype(BF16)
    kpet_ref[0] = zt[_T_KPE:_T_KPESW] * cost_ref[...] + zt[_T_KPESW:_T_FF] * sint_ref[...]

    ff = zt[_T_FF:_T_END] + bff_ref[...]
    logf = jnp.minimum(ff, 0.0) - jnp.log1p(jnp.exp(-jnp.abs(ff)))
    logft_ref[0] = logf

    @pl.when(t == 0)
    def _():
        carry_ref[...] = jnp.zeros_like(carry_ref)
        kmt_ref[...] = jnp.zeros_like(kmt_ref)

    row = lax.broadcasted_iota(jnp.int32, (tq, tq), 0)
    col = lax.broadcasted_iota(jnp.int32, (tq, tq), 1)
    upper = (row <= col).astype(BF16)
    cum = _dot3(logf, upper) + carry_ref[:, 0:1]
    cumt_ref[0] = cum
    cum_tm = jnp.transpose(cum)
    carry_ref[...] = jnp.broadcast_to(cum[:, tq - 1:tq], carry_ref.shape)

    lane = lax.broadcasted_iota(jnp.int32, (tq, 128), 1)
    for hh in range(FOX_HEADS):
        c1, c2, c3 = [c.astype(F32) for c in _split3(cum_tm[:, hh:hh + 1])]
        first = HEAD_DIM + 3 * (hh % (FOX_HEADS // FOX_KV_HEADS))
        extra = jnp.where((lane >= first) & (lane < first + 3), 1.0,
                          jnp.where(lane == HEAD_DIM + 6, c1,
                                    jnp.where(lane == HEAD_DIM + 7, c2, jnp.where(lane == HEAD_DIM + 8, c3, 0.0))))
        wide = z[:, _Z_FQ + HEAD_DIM * hh:_Z_FQ + HEAD_DIM * hh + 128] * qk_scale
        fq_ref[0, hh] = jnp.where(lane < HEAD_DIM, wide, extra)[:, 0:FOX_AUG].astype(BF16)
    for hh in range(MOBA_HEADS):
        wide = zmq[:, HEAD_DIM * hh:HEAD_DIM * hh + 128] * qk_scale
        mq_ref[0, hh] = jnp.where(lane < HEAD_DIM, wide, 0.0)

    for g in range(FOX_KV_HEADS):
        fkt_ref[0, g] = zt[_T_FK + HEAD_DIM * g:_T_FK + HEAD_DIM * (g + 1)]
        fvt_ref[0, g] = zt[_T_FV + HEAD_DIM * g:_T_FV + HEAD_DIM * (g + 1)]
    for g in range(MOBA_KV_HEADS):
        mkt_ref[0, g] = zmk[HEAD_DIM * g:HEAD_DIM * (g + 1)]
        mvt_ref[0, g] = zt[_T_MV + HEAD_DIM * g:_T_MV + HEAD_DIM * (g + 1)]

    bw = tq // nblk
    blane = lax.broadcasted_iota(jnp.int32, (HEAD_DIM, 128), 1)
    for g in range(MOBA_KV_HEADS):
        cur = kmt_ref[0, g]
        for jj in range(nblk):
            blk = zmk[HEAD_DIM * g:HEAD_DIM * (g + 1), bw * jj:bw * (jj + 1)]
            mean = jnp.sum(blk, axis=1, keepdims=True) * (1.0 / bw)
            cur = jnp.where(blane == t * nblk + jj, mean, cur)
        kmt_ref[0, g] = cur


def _project(x, lw, tables, tq):
    bp, t, d = x.shape
    nt = t // tq
    nblk = max(tq // MOBA_BLOCK, 1)
    assert nt * nblk <= MOBA_MAX_BLOCKS
    cos, sin, cost, sint = tables
    const = lambda *shape: pl.BlockSpec(shape, lambda b, i: (0,) * len(shape))
    tok = lambda *shape: pl.BlockSpec((1, tq) + shape, lambda b, i: (b, i) + (0,) * len(shape))
    headmajor = lambda nh, w: pl.BlockSpec((1, nh, tq, w), lambda b, i: (b, 0, i, 0))
    trans = lambda nh, w: pl.BlockSpec((1, nh, w, tq), lambda b, i: (b, 0, 0, i))
    trans2 = lambda w: pl.BlockSpec((1, w, tq), lambda b, i: (b, 0, i))
    in_specs = [
        tok(d), const(1, d), const(_Z_END, d), const(_T_END, d),
        const(MOBA_HEADS * HEAD_DIM + HEAD_DIM, d), const(MOBA_KV_HEADS * HEAD_DIM, d), const(8, 1),
        const(1, MLA_Q_LORA), const(1, MLA_KV_LORA), const(MLA_Q_LORA, 1024), const(MLA_HEADS // 2, 128, 256),
        pl.BlockSpec((tq, 256), lambda b, i: (i, 0)), pl.BlockSpec((tq, 256), lambda b, i: (i, 0)),
        pl.BlockSpec((MLA_ROPE, tq), lambda b, i: (0, i)), pl.BlockSpec((MLA_ROPE, tq), lambda b, i: (0, i)),
    ]
    out_shape = [
        jax.ShapeDtypeStruct((bp, MLA_HEADS, t, MLA_QK), BF16),
        jax.ShapeDtypeStruct((bp, t, MLA_QK), BF16),
        jax.ShapeDtypeStruct((bp, t, MLA_KV_LORA), F32),
        jax.ShapeDtypeStruct((bp, MLA_ROPE, t), F32),
        jax.ShapeDtypeStruct((bp, FOX_HEADS, t, FOX_AUG), BF16),
        jax.ShapeDtypeStruct((bp, MOBA_HEADS, t, MOBA_AUG), F32),
        jax.ShapeDtypeStruct((bp, FOX_KV_HEADS, HEAD_DIM, t), F32),
        jax.ShapeDtypeStruct((bp, FOX_KV_HEADS, HEAD_DIM, t), F32),
        jax.ShapeDtypeStruct((bp, MOBA_KV_HEADS, HEAD_DIM, t), F32),
        jax.ShapeDtypeStruct((bp, MOBA_KV_HEADS, HEAD_DIM, t), F32),
        jax.ShapeDtypeStruct((bp, 8, t), F32),
        jax.ShapeDtypeStruct((bp, 8, t), F32),
        jax.ShapeDtypeStruct((bp, MOBA_KV_HEADS, HEAD_DIM, 128), F32),
    ]
    out_specs = [
        headmajor(MLA_HEADS, MLA_QK), tok(MLA_QK), tok(MLA_KV_LORA), trans2(MLA_ROPE),
        headmajor(FOX_HEADS, FOX_AUG), headmajor(MOBA_HEADS, MOBA_AUG),
        trans(FOX_KV_HEADS, HEAD_DIM), trans(FOX_KV_HEADS, HEAD_DIM),
        trans(MOBA_KV_HEADS, HEAD_DIM), trans(MOBA_KV_HEADS, HEAD_DIM),
        trans2(8), trans2(8),
        pl.BlockSpec((1, MOBA_KV_HEADS, HEAD_DIM, 128), lambda b, i: (b, 0, 0, 0)),
    ]
    return pl.pallas_call(
        functools.partial(_proj_kernel, tq=tq, nblk=nblk),
        grid=(bp, nt), in_specs=in_specs, out_specs=out_specs, out_shape=out_shape,
        scratch_shapes=[pltpu.VMEM((8, 128), F32)],
        compiler_params=_params(("parallel", "arbitrary")), name="proj",
    )(x, lw["g_pre"], lw["w_tm"], lw["w_tr"], lw["w_mq"], lw["w_mk"], lw["b_ff"], lw["g_q"], lw["g_kv"], lw["w_uq"], lw["w_uk"],
      cos, sin, cost, sint)


def _online_update(carry, s, pv):
    m, l, acc = carry
    m_new = jnp.maximum(m, jnp.max(s, axis=-1, keepdims=True))
    alpha = jnp.exp(m - m_new)
    p = jnp.exp(s - m_new)
    l = alpha * l + jnp.sum(p, axis=-1, keepdims=True)
    acc = alpha * acc + pv(p.astype(BF16))
    return m_new, l, acc


def _mla_attn_kernel(q_ref, k_ref, o_ref, *, tq, tk):
    i = pl.program_id(1)
    nh = q_ref.shape[1]
    rows = nh * tq
    q = q_ref[0].reshape(rows, MLA_QK)
    nfull = (i * tq) // tk
    nall = ((i + 1) * tq + tk - 1) // tk

    def step(j, carry, masked):
        off = pl.multiple_of(j * tk, tk)
        k = k_ref[0, pl.ds(off, tk), :]
        s = _nt(q, k)
        if masked:
            qpos = i * tq + (lax.broadcasted_iota(jnp.int32, (rows, tk), 0) & (tq - 1))
            kpos = off + lax.broadcasted_iota(jnp.int32, (rows, tk), 1)
            s = jnp.where(kpos <= qpos, s, NEG)
        return _online_update(carry, s, lambda p: _nn(p, k[:, 0:MLA_KV_LORA]))

    init = (jnp.full((rows, 1), NEG, F32), jnp.zeros((rows, 1), F32), jnp.zeros((rows, MLA_KV_LORA), F32))
    carry = lax.fori_loop(0, nfull, functools.partial(step, masked=False), init)
    _, l, acc = lax.fori_loop(nfull, nall, functools.partial(step, masked=True), carry)
    o = (acc / l).astype(o_ref.dtype)
    for hh in range(nh):
        o_ref[0, :, MLA_KV_LORA * hh:MLA_KV_LORA * (hh + 1)] = o[hh * tq:(hh + 1) * tq]


def _mla_attn(qmla, kcat, tq, tk):
    bp, nh, t, _ = qmla.shape
    return pl.pallas_call(
        functools.partial(_mla_attn_kernel, tq=tq, tk=tk),
        grid=(bp, t // tq),
        in_specs=[pl.BlockSpec((1, nh, tq, MLA_QK), lambda b, i: (b, 0, i, 0)),
                  pl.BlockSpec((1, t, MLA_QK), lambda b, i: (b, 0, 0))],
        out_specs=pl.BlockSpec((1, tq, nh * MLA_KV_LORA), lambda b, i: (b, i, 0)),
        out_shape=jax.ShapeDtypeStruct((bp, t, nh * MLA_KV_LORA), BF16),
        compiler_params=_params(("parallel", "arbitrary")), name="mla_attn",
    )(qmla, kcat)


def _causal_flash(q, k_tile, v_tile, i, tq, tk):
    jd = (i * tq) // tk


    def step(j, carry, masked):
        m, acc = carry
        s = _nn(q, k_tile(j))
        if masked:
            qpos = i * tq + (lax.broadcasted_iota(jnp.int32, s.shape, 0) & (tq - 1))
            kpos = j * tk + lax.broadcasted_iota(jnp.int32, s.shape, 1)
            s = jnp.where(kpos <= qpos, s, NEG)
        m_new = jnp.maximum(m, jnp.max(s, axis=-1, keepdims=True))
        p = jnp.exp(s - m_new).astype(BF16)
        return m_new, jnp.exp(m - m_new) * acc + _nt(p, v_tile(j))

    rows = q.shape[0]
    init = (jnp.full((rows, 1), NEG, F32), jnp.zeros((rows, V_AUG), F32))
    carry = step(jd, init, True)
    _, acc = lax.fori_loop(0, jd, functools.partial(step, masked=False), carry)
    return acc[:, 0:HEAD_DIM] / acc[:, HEAD_DIM:HEAD_DIM + 1]


def _store_values_with_ones(vt_ref, vb_ref):
    seq = vb_ref.shape[2]
    row = lax.broadcasted_iota(jnp.int32, (V_AUG - HEAD_DIM, seq), 0)
    ones_row = jnp.where(row == 0, 1.0, 0.0).astype(BF16)
    for g in range(vb_ref.shape[0]):
        vb_ref[g, 0:HEAD_DIM, :] = vt_ref[0, g].astype(BF16)
        vb_ref[g, HEAD_DIM:V_AUG, :] = ones_row


def _fox_attn_kernel(q_ref, kt_ref, vt_ref, cumt_ref, o_ref, ka_ref, vb_ref, *, tq, tk):
    i = pl.program_id(1)
    grp = FOX_HEADS // FOX_KV_HEADS
    seq = ka_ref.shape[2]

    @pl.when(i == 0)
    def _():
        _store_values_with_ones(vt_ref, vb_ref)
        row = lax.broadcasted_iota(jnp.int32, (FOX_AUG - HEAD_DIM, seq), 0)
        for g in range(FOX_KV_HEADS):
            ka_ref[g, 0:HEAD_DIM, :] = kt_ref[0, g].astype(BF16)
            extra = jnp.where((row >= 6) & (row < 9), 1.0, 0.0)
            for e in range(grp):
                parts = _split3(cumt_ref[0, grp * g + e:grp * g + e + 1, :])
                for n, c in enumerate(parts):
                    extra = jnp.where(row == 3 * e + n, -c.astype(F32), extra)
            ka_ref[g, HEAD_DIM:FOX_AUG, :] = extra.astype(BF16)

    for g in range(FOX_KV_HEADS):
        q = q_ref[0, grp * g:grp * (g + 1)].reshape(grp * tq, FOX_AUG)
        k_tile = lambda j, g=g: ka_ref[g, :, pl.ds(pl.multiple_of(j * tk, tk), tk)]
        v_tile = lambda j, g=g: vb_ref[g, :, pl.ds(pl.multiple_of(j * tk, tk), tk)]
        o = _causal_flash(q, k_tile, v_tile, i, tq, tk)
        for e in range(grp):
            hh = grp * g + e
            o_ref[0, :, HEAD_DIM * hh:HEAD_DIM * (hh + 1)] = o[e * tq:(e + 1) * tq]


def _fox_attn(fq, fkt, fvt, cumt, tq, tk):
    bp, nh, t, _ = fq.shape
    assert tk % tq == 0
    kv_spec = pl.BlockSpec((1, FOX_KV_HEADS, HEAD_DIM, t), lambda b, i: (b, 0, 0, 0))
    return pl.pallas_call(
        functools.partial(_fox_attn_kernel, tq=tq, tk=tk),
        grid=(bp, t // tq),
        in_specs=[pl.BlockSpec((1, nh, tq, FOX_AUG), lambda b, i: (b, 0, i, 0)), kv_spec, kv_spec,
                  pl.BlockSpec((1, 8, t), lambda b, i: (b, 0, 0))],
        out_specs=pl.BlockSpec((1, tq, nh * HEAD_DIM), lambda b, i: (b, i, 0)),
        out_shape=jax.ShapeDtypeStruct((bp, t, nh * HEAD_DIM), F32),
        scratch_shapes=[pltpu.VMEM((FOX_KV_HEADS, FOX_AUG, t), BF16), pltpu.VMEM((FOX_KV_HEADS, V_AUG, t), BF16)],
        compiler_params=_params(("parallel", "arbitrary")), name="fox_attn",
    )(fq, fkt, fvt, cumt)


def _alibi_slope(hh, n):
    return float(2.0 ** (-8.0 * (hh + 1) / n))


def _top_blocks(gate, eligible, lane):
    g = jnp.where(eligible, gate, NEG)
    sel = jnp.zeros(g.shape, jnp.bool_)
    big = g.shape[-1]
    for _ in range(MOBA_TOPK):
        mx = jnp.max(g, axis=-1, keepdims=True)
        cand = (g == mx) & (g > 0.5 * NEG)
        idx = jnp.min(jnp.where(cand, lane, big), axis=-1, keepdims=True)
        pick = lane == idx
        sel = sel | pick
        g = jnp.where(pick, NEG, g)
    return sel


_MOBA_FILL = 256


def _moba_attn_kernel(q_ref, kt_ref, vt_ref, kmt_ref, o_ref, ka_ref, vb_ref, *, tq, tk):
    i = pl.program_id(1)
    grp = MOBA_HEADS // MOBA_KV_HEADS
    seq = ka_ref.shape[2]
    nfeat = MOBA_AUG - HEAD_DIM

    @pl.when(i == 0)
    def _():
        _store_values_with_ones(vt_ref, vb_ref)

        def fill(c, _):
            off = pl.multiple_of(c * _MOBA_FILL, _MOBA_FILL)
            kpos = off + lax.broadcasted_iota(jnp.int32, (nfeat, _MOBA_FILL), 1)
            row = lax.broadcasted_iota(jnp.int32, (nfeat, _MOBA_FILL), 0)
            onehot = (kpos // MOBA_BLOCK) == (row - 4)
            feat = jnp.where(row == 0, kpos // 64, jnp.where(row == 1, kpos % 64, jnp.where((row < 4) | onehot, 1, 0)))
            feat = feat.astype(F32).astype(BF16)
            for g in range(MOBA_KV_HEADS):
                ka_ref[g, 0:HEAD_DIM, pl.ds(off, _MOBA_FILL)] = kt_ref[0, g, :, pl.ds(off, _MOBA_FILL)].astype(BF16)
                ka_ref[g, HEAD_DIM:MOBA_AUG, pl.ds(off, _MOBA_FILL)] = feat
            return 0

        lax.fori_loop(0, seq // _MOBA_FILL, fill, 0)

    lane = lax.broadcasted_iota(jnp.int32, (tq, MOBA_AUG), 1)
    qpos = i * tq + lax.broadcasted_iota(jnp.int32, (tq, MOBA_AUG), 0)
    def extended_query(hh):
        g = hh // grp
        slope = _alibi_slope(hh, MOBA_HEADS)
        q32 = q_ref[0, hh]
        gate = _nn_f32(q32[:, 0:HEAD_DIM], kmt_ref[0, g])
        chosen = _top_blocks(gate, lane < i, lane) | (lane == i)
        penalty = pltpu.roll(jnp.where(chosen, 0.0, NEG), HEAD_DIM + 4, axis=1)
        alibi = jnp.where(lane == HEAD_DIM, 64.0 * slope,
                          jnp.where(lane == HEAD_DIM + 1, slope,
                                    jnp.where(lane == HEAD_DIM + 2, (-64.0 * slope) * (qpos // 64).astype(F32),
                                              -slope * (qpos % 64).astype(F32))))
        return jnp.where(lane < HEAD_DIM, q32, jnp.where(lane < HEAD_DIM + 4, alibi, penalty)).astype(BF16)

    for g in range(MOBA_KV_HEADS):
        q_aug = jnp.concatenate([extended_query(grp * g + e) for e in range(grp)], axis=0)
        k_tile = lambda j, g=g: ka_ref[g, :, pl.ds(pl.multiple_of(j * tk, tk), tk)]
        v_tile = lambda j, g=g: vb_ref[g, :, pl.ds(pl.multiple_of(j * tk, tk), tk)]
        o = _causal_flash(q_aug, k_tile, v_tile, i, tq, tk)
        for e in range(grp):
            hh = grp * g + e
            o_ref[0, :, HEAD_DIM * hh:HEAD_DIM * (hh + 1)] = o[e * tq:(e + 1) * tq]


def _moba_attn(mq, mkt, mvt, kmt, tk):
    bp, nh, t, _ = mq.shape
    tq = MOBA_BLOCK
    assert tk % tq == 0 and t % _MOBA_FILL == 0 and t // MOBA_BLOCK <= MOBA_MAX_BLOCKS
    kv_spec = pl.BlockSpec((1, MOBA_KV_HEADS, HEAD_DIM, t), lambda b, i: (b, 0, 0, 0))
    return pl.pallas_call(
        functools.partial(_moba_attn_kernel, tq=tq, tk=tk),
        grid=(bp, t // tq),
        in_specs=[pl.BlockSpec((1, nh, tq, MOBA_AUG), lambda b, i: (b, 0, i, 0)), kv_spec, kv_spec,
                  pl.BlockSpec((1, MOBA_KV_HEADS, HEAD_DIM, 128), lambda b, i: (b, 0, 0, 0))],
        out_specs=pl.BlockSpec((1, tq, nh * HEAD_DIM), lambda b, i: (b, i, 0)),
        out_shape=jax.ShapeDtypeStruct((bp, t, nh * HEAD_DIM), F32),
        scratch_shapes=[pltpu.VMEM((MOBA_KV_HEADS, MOBA_AUG, t), BF16), pltpu.VMEM((MOBA_KV_HEADS, V_AUG, t), BF16)],
        compiler_params=_params(("parallel", "arbitrary")), name="moba_attn",
    )(mq, mkt, mvt, kmt)


class _PagedStream:
    def __init__(self, pt_ref, copies, n_pages, pp, reverse):
        self.pt_ref, self.copies, self.pp = pt_ref, copies, pp
        self.nc = n_pages // pp
        self.n_pages = n_pages
        self.reverse = reverse

    def chunk_of(self, step):
        seq = step // self.nc
        c = step % self.nc
        if self.reverse:
            c = self.nc - 1 - c
        return seq * self.n_pages + c * self.pp, c

    def _descs(self, step, slot):
        base, _ = self.chunk_of(step)
        out = []
        for pg in range(self.pp):
            out.extend(self.copies(self.pt_ref[base + pg], slot, pg))
        return out

    def start(self, step, slot):
        for d in self._descs(step, slot):
            d.start()

    def wait(self, step, slot):
        for d in self._descs(step, slot):
            d.wait()

    def run(self, body, init):
        b = pl.program_id(0)
        total = pl.num_programs(0) * self.nc

        @pl.when(b == 0)
        def _():
            self.start(0, 0)

        def loop(cc, carry):
            step = b * self.nc + cc
            slot = step % 2

            @pl.when(step + 1 < total)
            def _():
                self.start(step + 1, 1 - slot)

            self.wait(step, slot)
            _, c = self.chunk_of(step)
            return body(c, slot, carry)

        return lax.fori_loop(0, self.nc, loop, init)


def _mla_dec_kernel(pt_ref, q_ref, knew_ref, ckv_hbm, kpet_hbm, o_ref, ckv_buf, kpet_buf, sem,
                    *, layer, n_pages, pp):
    n = pp * PAGE_SIZE

    def copies(page, slot, pg):
        return [
            pltpu.make_async_copy(ckv_hbm.at[layer, page],
                                  ckv_buf.at[slot, pl.ds(pg * PAGE_SIZE, PAGE_SIZE)], sem.at[0, slot]),
            pltpu.make_async_copy(kpet_hbm.at[layer, page],
                                  kpet_buf.at[slot, :, pl.ds(pg * PAGE_SIZE, PAGE_SIZE)], sem.at[1, slot]),
        ]

    stream = _PagedStream(pt_ref, copies, n_pages, pp, reverse=False)
    q = q_ref[0]
    knew = knew_ref[0]
    s_new = jnp.sum(q.astype(F32) * knew.astype(F32), axis=-1, keepdims=True)
    v_new = jnp.broadcast_to(knew[:, 0:MLA_KV_LORA].astype(F32), (MLA_HEADS, MLA_KV_LORA))

    def body(c, slot, carry):
        kv = ckv_buf[slot].astype(BF16)
        kt = kpet_buf[slot].astype(BF16)
        s = _nt(q[:, 0:MLA_KV_LORA], kv) + _nn(q[:, MLA_KV_LORA:MLA_QK], kt)
        return _online_update(carry, s, lambda p: _nn(p, kv))

    init = (s_new, jnp.ones((MLA_HEADS, 1), F32), v_new)
    _, l, acc = stream.run(body, init)
    o_ref[0] = (acc / l).astype(o_ref.dtype)


def _mla_dec(pt, q, knew, cache_ckv, cache_kpet, layer, pp):
    db = q.shape[0]
    n_pages = pt.shape[0] // db
    n = pp * PAGE_SIZE
    grid_spec = pltpu.PrefetchScalarGridSpec(
        num_scalar_prefetch=1, grid=(db,),
        in_specs=[pl.BlockSpec((1, MLA_HEADS, MLA_QK), lambda b, pt: (b, 0, 0)),
                  pl.BlockSpec((1, 1, MLA_QK), lambda b, pt: (b, 0, 0)),
                  pl.BlockSpec(memory_space=pl.ANY), pl.BlockSpec(memory_space=pl.ANY)],
        out_specs=pl.BlockSpec((1, MLA_HEADS, MLA_KV_LORA), lambda b, pt: (b, 0, 0)),
        scratch_shapes=[pltpu.VMEM((2, n, MLA_KV_LORA), F32), pltpu.VMEM((2, MLA_ROPE, n), F32),
                        pltpu.SemaphoreType.DMA((2, 2))])
    return pl.pallas_call(
        functools.partial(_mla_dec_kernel, layer=layer, n_pages=n_pages, pp=pp),
        grid_spec=grid_spec, out_shape=jax.ShapeDtypeStruct((db, MLA_HEADS, MLA_KV_LORA), BF16),
        compiler_params=_params(("arbitrary",)), name="mla_dec",
    )(pt, q, knew, cache_ckv, cache_kpet)


def _rows_by_group(parts, grp):
    out = parts[-1]
    rowid = lax.broadcasted_iota(jnp.int32, out.shape, 0)
    for g in range(len(parts) - 2, -1, -1):
        out = jnp.where(rowid < (g + 1) * grp, parts[g], out)
    return out


def _fox_dec_kernel(pt_ref, q_ref, knew_ref, vnew_ref, lfnew_ref, kt_hbm, vt_hbm, lft_hbm, o_ref,
                    kt_buf, vt_buf, lf_buf, sem, *, layer, n_pages, pp):
    b = pl.program_id(0)
    grp = FOX_HEADS // FOX_KV_HEADS

    def copies(page, slot, pg):
        return [
            pltpu.make_async_copy(kt_hbm.at[layer, page],
                                  kt_buf.at[slot, :, :, pl.ds(pg * PAGE_SIZE, PAGE_SIZE)], sem.at[0, slot]),
            pltpu.make_async_copy(vt_hbm.at[layer, page],
                                  vt_buf.at[slot, :, :, pl.ds(pg * PAGE_SIZE, PAGE_SIZE)], sem.at[1, slot]),
            pltpu.make_async_copy(lft_hbm.at[layer, page],
                                  lf_buf.at[slot, pg, 0:FOX_HEADS], sem.at[2, slot]),
        ]

    @pl.when(b == 0)
    def _():
        lf_buf[...] = jnp.zeros_like(lf_buf)

    stream = _PagedStream(pt_ref, copies, n_pages, pp, reverse=True)
    q = q_ref[0]
    s_new = jnp.sum(q.astype(F32) * knew_ref[0], axis=-1, keepdims=True)
    row = lax.broadcasted_iota(jnp.int32, (PAGE_SIZE, PAGE_SIZE), 0)
    col = lax.broadcasted_iota(jnp.int32, (PAGE_SIZE, PAGE_SIZE), 1)
    later = (row > col).astype(BF16)

    def body(c, slot, carry):
        m, l, acc, run = carry
        lf = lf_buf[slot].reshape(pp * 8, PAGE_SIZE)
        within = _dot3(lf, later)
        tot = jnp.sum(lf, axis=-1, keepdims=True)
        bias = [None] * pp
        for pg in range(pp - 1, -1, -1):
            bias[pg] = within[8 * pg:8 * (pg + 1)] + run
            run = run + tot[8 * pg:8 * (pg + 1)]
        bias = jnp.concatenate(bias, axis=-1)
        kt = kt_buf[slot].astype(BF16)
        vt = vt_buf[slot].astype(BF16)
        s = _rows_by_group([_nn(q, kt[g]) for g in range(FOX_KV_HEADS)], grp) + bias
        pv = lambda p: _rows_by_group([_nt(p, vt[g]) for g in range(FOX_KV_HEADS)], grp)
        m, l, acc = _online_update((m, l, acc), s, pv)
        return m, l, acc, run

    init = (s_new, jnp.ones((8, 1), F32), vnew_ref[0], lfnew_ref[0])
    _, l, acc, _ = stream.run(body, init)
    o_ref[0] = acc / l


def _kv_dec_specs(db, with_lf):
    specs = [pl.BlockSpec((1, 8, HEAD_DIM), lambda b, pt: (b, 0, 0))] * 3
    if with_lf:
        specs.append(pl.BlockSpec((1, 8, 1), lambda b, pt: (b, 0, 0)))
    return specs


def _fox_dec(pt, q, knew, vnew, lfnew, cache_kt, cache_vt, cache_lft, layer, pp):
    db = q.shape[0]
    n_pages = pt.shape[0] // db
    n = pp * PAGE_SIZE
    grid_spec = pltpu.PrefetchScalarGridSpec(
        num_scalar_prefetch=1, grid=(db,),
        in_specs=_kv_dec_specs(db, True) + [pl.BlockSpec(memory_space=pl.ANY)] * 3,
        out_specs=pl.BlockSpec((1, 8, HEAD_DIM), lambda b, pt: (b, 0, 0)),
        scratch_shapes=[pltpu.VMEM((2, FOX_KV_HEADS, HEAD_DIM, n), F32),
                        pltpu.VMEM((2, FOX_KV_HEADS, HEAD_DIM, n), F32),
                        pltpu.VMEM((2, pp, 8, PAGE_SIZE), F32),
                        pltpu.SemaphoreType.DMA((3, 2))])
    return pl.pallas_call(
        functools.partial(_fox_dec_kernel, layer=layer, n_pages=n_pages, pp=pp),
        grid_spec=grid_spec, out_shape=jax.ShapeDtypeStruct((db, 8, HEAD_DIM), F32),
        compiler_params=_params(("arbitrary",)), name="fox_dec",
    )(pt, q, knew, vnew, lfnew, cache_kt, cache_vt, cache_lft)


def _moba_dec_kernel(pt_ref, q_ref, knew_ref, vnew_ref, kt_hbm, vt_hbm, o_ref,
                     kt_buf, vt_buf, ksum_ref, m_ref, l_ref, acc_ref, sem, *, layer, n_pages, pp):
    grp = MOBA_HEADS // MOBA_KV_HEADS
    n = pp * PAGE_SIZE
    bpc = n // MOBA_BLOCK
    nblocks = n_pages * PAGE_SIZE // MOBA_BLOCK
    past = n_pages * PAGE_SIZE

    def copies(page, slot, pg):
        return [
            pltpu.make_async_copy(kt_hbm.at[layer, page],
                                  kt_buf.at[slot, :, :, pl.ds(pg * PAGE_SIZE, PAGE_SIZE)], sem.at[0, slot]),
            pltpu.make_async_copy(vt_hbm.at[layer, page],
                                  vt_buf.at[slot, :, :, pl.ds(pg * PAGE_SIZE, PAGE_SIZE)], sem.at[1, slot]),
        ]

    stream = _PagedStream(pt_ref, copies, n_pages, pp, reverse=False)
    q32 = q_ref[0]
    q = q32.astype(BF16)
    rowid = lax.broadcasted_iota(jnp.int32, (8, 1), 0)
    slopes = jnp.zeros((8, 1), F32)
    for hh in range(MOBA_HEADS):
        slopes = jnp.where(rowid == hh, _alibi_slope(hh, MOBA_HEADS), slopes)
    lanepos = lax.broadcasted_iota(jnp.int32, (8, n), 1)
    blklane = lax.broadcasted_iota(jnp.int32, (HEAD_DIM, 128), 1)
    ksum_ref[...] = jnp.zeros_like(ksum_ref)

    def body(c, slot, carry):
        kt = kt_buf[slot].astype(BF16)
        vt = vt_buf[slot].astype(BF16)
        raw = _rows_by_group([_nn(q, kt[g]) for g in range(MOBA_KV_HEADS)], grp)
        dist = (past - c * n - lanepos).astype(F32)
        s = raw - slopes * dist
        for jj in range(bpc):
            lo, hi = jj * MOBA_BLOCK, (jj + 1) * MOBA_BLOCK
            blk = c * bpc + jj
            for g in range(MOBA_KV_HEADS):
                ksum = jnp.sum(kt_buf[slot, g, :, lo:hi], axis=-1, keepdims=True)
                ksum_ref[g] = jnp.where(blklane == blk, ksum, ksum_ref[g])
            sb = s[:, lo:hi]
            m = jnp.max(sb, axis=-1, keepdims=True)
            p = jnp.exp(sb - m)
            l = jnp.sum(p, axis=-1, keepdims=True)
            pb = p.astype(BF16)
            acc = _rows_by_group([_nt(pb, vt[g][:, lo:hi]) for g in range(MOBA_KV_HEADS)], grp)
            m_ref[blk] = jnp.broadcast_to(m, (8, 128))
            l_ref[blk] = jnp.broadcast_to(l, (8, 128))
            acc_ref[blk] = acc
        return carry

    stream.run(body, 0)

    gate = _rows_by_group([_nn_f32(q32, ksum_ref[g]) for g in range(MOBA_KV_HEADS)], grp)
    g = jnp.stack([jnp.broadcast_to(gate[:, bb:bb + 1], (8, 128)) for bb in range(nblocks)])
    blkid = lax.broadcasted_iota(jnp.int32, g.shape, 0)
    sel = jnp.zeros(g.shape, jnp.bool_)
    for _ in range(min(MOBA_TOPK, nblocks)):
        mx = jnp.max(g, axis=0, keepdims=True)
        cand = (g == mx) & (g > 0.5 * NEG)
        idx = jnp.min(jnp.where(cand, blkid, nblocks), axis=0, keepdims=True)
        pick = blkid == idx
        sel = sel | pick
        g = jnp.where(pick, NEG, g)
    s_new = jnp.sum(q.astype(F32) * knew_ref[0], axis=-1, keepdims=True)
    mb = m_ref[...]
    mtop = jnp.maximum(jnp.max(jnp.where(sel, mb, NEG), axis=0), s_new)
    w = jnp.where(sel, jnp.exp(mb - mtop[None]), 0.0)
    w_new = jnp.exp(s_new - mtop)
    denom = jnp.sum(w * l_ref[...], axis=0) + w_new
    num = jnp.sum(w[:, :, 0:HEAD_DIM] * acc_ref[...], axis=0) + w_new[:, 0:HEAD_DIM] * vnew_ref[0]
    o_ref[0] = num / denom[:, 0:HEAD_DIM]


def _moba_dec(pt, q, knew, vnew, cache_kt, cache_vt, layer, pp):
    db = q.shape[0]
    n_pages = pt.shape[0] // db
    n = pp * PAGE_SIZE
    nblocks = n_pages * PAGE_SIZE // MOBA_BLOCK
    assert nblocks <= 128
    grid_spec = pltpu.PrefetchScalarGridSpec(
        num_scalar_prefetch=1, grid=(db,),
        in_specs=_kv_dec_specs(db, False) + [pl.BlockSpec(memory_space=pl.ANY)] * 2,
        out_specs=pl.BlockSpec((1, 8, HEAD_DIM), lambda b, pt: (b, 0, 0)),
        scratch_shapes=[pltpu.VMEM((2, MOBA_KV_HEADS, HEAD_DIM, n), F32),
                        pltpu.VMEM((2, MOBA_KV_HEADS, HEAD_DIM, n), F32),
                        pltpu.VMEM((MOBA_KV_HEADS, HEAD_DIM, 128), F32), pltpu.VMEM((nblocks, 8, 128), F32),
                        pltpu.VMEM((nblocks, 8, 128), F32), pltpu.VMEM((nblocks, 8, HEAD_DIM), F32),
                        pltpu.SemaphoreType.DMA((2, 2))])
    return pl.pallas_call(
        functools.partial(_moba_dec_kernel, layer=layer, n_pages=n_pages, pp=pp),
        grid_spec=grid_spec, out_shape=jax.ShapeDtypeStruct((db, 8, HEAD_DIM), F32),
        compiler_params=_params(("arbitrary",)), name="moba_dec",
    )(pt, q, knew, vnew, cache_kt, cache_vt)


def _mix_kernel(olat_ref, ofox_ref, omoba_ref, x_ref, wuv_ref, gn_ref, wout_ref, gpost_ref, gffn_ref,
                x1_ref, h2_ref):
    n_mla = MLA_HEADS * MLA_V
    n_fox = FOX_HEADS * HEAD_DIM
    gn = gn_ref[...]
    o_mla = _nn(olat_ref[...], wuv_ref[...])
    o = jnp.concatenate([_rms(o_mla, gn[:, 0:n_mla]),
                         _rms(ofox_ref[...], gn[:, n_mla:n_mla + n_fox]),
                         _rms(omoba_ref[...], gn[:, n_mla + n_fox:])], axis=-1).astype(BF16)
    x1 = x_ref[...] + _rms(_nn(o, wout_ref[...]), gpost_ref[...])
    x1_ref[...] = x1
    h2_ref[...] = _rms(x1, gffn_ref[...]).astype(BF16)


def _mix(olat, ofox, omoba, x, lw, tq):
    t, d = x.shape
    row = lambda w: pl.BlockSpec((tq, w), lambda i: (i, 0))
    const = lambda *shape: pl.BlockSpec(shape, lambda i: (0,) * len(shape))
    return pl.pallas_call(
        _mix_kernel, grid=(t // tq,),
        in_specs=[row(olat.shape[1]), row(ofox.shape[1]), row(omoba.shape[1]), row(d),
                  const(*lw["w_uv"].shape), const(1, d), const(*lw["w_out"].shape), const(1, d), const(1, d)],
        out_specs=[row(d), row(d)],
        out_shape=[jax.ShapeDtypeStruct((t, d), F32), jax.ShapeDtypeStruct((t, d), BF16)],
        compiler_params=_params(("parallel",)), name="mix_out",
    )(olat, ofox, omoba, x, lw["w_uv"], lw["g_group"], lw["w_out"], lw["g_post"], lw["g_ffn_pre"])


def _ffn_kernel(x1_ref, h2_ref, ple_ref, wg_ref, wu_ref, wd_ref, gpost_ref, wpp_ref, wpg_ref, y_ref, acc_ref):
    j = pl.program_id(1)

    @pl.when(j == 0)
    def _():
        acc_ref[...] = jnp.zeros_like(acc_ref)

    h2 = h2_ref[...]
    gate = _nn(h2, wg_ref[...])
    act = (gate * jax.nn.sigmoid(gate)) * _nn(h2, wu_ref[...])
    acc_ref[...] += _nn(act.astype(BF16), wd_ref[...])

    @pl.when(j == pl.num_programs(1) - 1)
    def _():
        x2 = x1_ref[...] + _rms(acc_ref[...], gpost_ref[...])
        emb = _nn(ple_ref[...].astype(BF16), wpp_ref[...])
        y_ref[...] = x2 + emb * jax.nn.sigmoid(_nn(x2.astype(BF16), wpg_ref[...]))


def _ffn(x1, h2, ple, lw, tq, th):
    t, d = x1.shape
    hid = lw["w_gate"].shape[1]
    row = lambda w: pl.BlockSpec((tq, w), lambda i, j: (i, 0))
    const = lambda *shape: pl.BlockSpec(shape, lambda i, j: (0,) * len(shape))
    return pl.pallas_call(
        _ffn_kernel, grid=(t // tq, hid // th),
        in_specs=[row(d), row(d), row(ple.shape[1]),
                  pl.BlockSpec((d, th), lambda i, j: (0, j)), pl.BlockSpec((d, th), lambda i, j: (0, j)),
                  pl.BlockSpec((th, d), lambda i, j: (j, 0)),
                  const(1, d), const(*lw["w_ple_proj"].shape), const(*lw["w_ple_gate"].shape)],
        out_specs=row(d), out_shape=jax.ShapeDtypeStruct((t, d), F32),
        scratch_shapes=[pltpu.VMEM((tq, d), F32)],
        compiler_params=_params(("parallel", "arbitrary")), name="ffn",
    )(x1, h2, ple, lw["w_gate"], lw["w_up"], lw["w_down"], lw["g_ffn_post"], lw["w_ple_proj"], lw["w_ple_gate"])


def _rope_tables(pos):
    inv = 1.0 / (ROPE_THETA ** (np.arange(0, MLA_ROPE, 2, dtype=np.float32) / MLA_ROPE))
    ang = pos.astype(F32)[:, None] * jnp.asarray(inv, F32)
    c, s = jnp.cos(ang), jnp.sin(ang)
    cos32 = jnp.concatenate([c, c], axis=-1)
    sin32 = jnp.concatenate([-s, s], axis=-1)
    return (jnp.tile(cos32, (1, MLA_HEADS)), jnp.tile(sin32, (1, MLA_HEADS)),
            jnp.transpose(cos32), jnp.transpose(sin32))


def _layer_weights(i, w):
    d = w["w_in"].shape[1]
    wt = jnp.transpose(w["w_in"], (2, 0, 1))[:, i, :]
    o = np.cumsum([0, MLA_Q_LORA, MLA_KV_LORA, MLA_ROPE, FOX_HEADS * HEAD_DIM, FOX_KV_HEADS * HEAD_DIM,
                   FOX_KV_HEADS * HEAD_DIM, FOX_HEADS, MOBA_HEADS * HEAD_DIM, MOBA_KV_HEADS * HEAD_DIM,
                   MOBA_KV_HEADS * HEAD_DIM]).tolist()
    cq, ckv, kpe, fq, fk, fv, ff, mq, mk, mv = [wt[o[k]:o[k + 1]] for k in range(10)]
    half = MLA_ROPE // 2
    swap = np.concatenate([np.arange(half, MLA_ROPE), np.arange(0, half)])
    kpe_sw = kpe[swap]
    w_tm = jnp.concatenate([cq, ckv, fq, kpe, kpe_sw], axis=0).astype(BF16)
    w_tr = jnp.concatenate([fk, fv, mv, kpe, kpe_sw, ff, jnp.zeros((8 - FOX_HEADS, d), F32)], axis=0).astype(BF16)
    b_ff = jnp.concatenate([w["b_fox_f"][i], jnp.zeros((8 - FOX_HEADS,), F32)])[:, None]

    per = MLA_NOPE + MLA_ROPE
    heads = np.arange(MLA_HEADS)[:, None] * per
    nope_cols = (heads + np.arange(MLA_NOPE)[None]).reshape(-1)
    pe_cols = (heads + MLA_NOPE + np.arange(MLA_ROPE)[None]).reshape(-1)
    pesw_cols = (heads + MLA_NOPE + swap[None]).reshape(-1)
    w_uq = w["w_mla_uq"][i][:, np.concatenate([nope_cols, pe_cols, pesw_cols])].astype(BF16)

    uk = jnp.transpose(w["w_mla_uk"][i], (1, 2, 0))
    z = jnp.zeros((MLA_NOPE, MLA_KV_LORA), F32)
    w_uk = jnp.stack([jnp.concatenate([jnp.concatenate([uk[2 * j], z], axis=1),
                                       jnp.concatenate([z, uk[2 * j + 1]], axis=1)], axis=0)
                      for j in range(MLA_HEADS // 2)]).astype(BF16)
    uv = w["w_mla_uv"][i]
    w_uv = jnp.zeros((MLA_HEADS * MLA_KV_LORA, MLA_HEADS * MLA_V), F32)
    for hh in range(MLA_HEADS):
        w_uv = lax.dynamic_update_slice(w_uv, uv[:, hh, :], (hh * MLA_KV_LORA, hh * MLA_V))
    row = lambda a: a[i][None, :]
    return dict(
        g_pre=row(w["norm_mix_pre"]), w_tm=w_tm, w_tr=w_tr,
        w_mq=jnp.concatenate([mq, jnp.zeros((HEAD_DIM, d), F32)], axis=0), w_mk=mk, b_ff=b_ff, g_q=row(w["mla_q_norm"]),
        g_kv=row(w["mla_kv_norm"]), w_uq=w_uq, w_uk=w_uk, w_uv=w_uv.astype(BF16),
        g_group=row(w["group_norm"]), w_out=w["w_out"][i].astype(BF16), g_post=row(w["norm_mix_post"]),
        g_ffn_pre=row(w["norm_ffn_pre"]), g_ffn_post=row(w["norm_ffn_post"]),
        w_gate=w["w_ffn_gate"][i].astype(BF16), w_up=w["w_ffn_up"][i].astype(BF16),
        w_down=w["w_ffn_down"][i].astype(BF16), w_ple_proj=w["w_ple_proj"][i].astype(BF16),
        w_ple_gate=w["w_ple_gate"][i].astype(BF16))


def _pick(n, prefs):
    for p in prefs:
        if n % p == 0:
            return p
    return n


def _cache_rows(m):
    kv = lambda a: jnp.transpose(a, (0, 3, 1, 2))
    return (m["ckv"], jnp.transpose(m["kpet"], (0, 2, 1)), kv(m["fkt"]), kv(m["fvt"]),
            jnp.transpose(m["logft"][:, 0:FOX_HEADS], (0, 2, 1)), kv(m["mkt"]), kv(m["mvt"]))


_PROJ_NAMES = ("qmla", "kcat", "ckv", "kpet", "fq", "mq", "fkt", "fvt", "mkt", "mvt", "logft", "cumt", "kmt")


def _transposed_caches(caches):
    ckv, kpe, fk, fv, lf, mk, mv = caches
    kv = lambda a: jnp.transpose(a, (0, 1, 3, 4, 2))
    return (ckv, jnp.transpose(kpe, (0, 1, 3, 2)), kv(fk), kv(fv), jnp.transpose(lf, (0, 1, 3, 2)), kv(mk), kv(mv))


def _sample_attention(ys, lw, layer, page_table, caches_t):
    cache_ckv, cache_kpet, cache_fkt, cache_fvt, cache_lft, cache_mkt, cache_mvt = caches_t
    db = ys.shape[1]
    n_pages = page_table.shape[1]
    pt_flat = page_table.reshape(-1)
    pp = _pick(n_pages, (64, 32, 16, 8, 4, 2))
    tables_s = _rope_tables(jnp.full((db,), n_pages * PAGE_SIZE, jnp.int32))
    pad_heads = lambda a: jnp.concatenate([a, jnp.zeros((db, 8 - a.shape[1]) + a.shape[2:], a.dtype)], axis=1)
    grp_rows = np.array([0, 0, 1, 1, 0, 0, 0, 0])
    tokmajor = lambda a: jnp.transpose(a[0], (2, 0, 1))

    ms = dict(zip(_PROJ_NAMES, _project(ys, lw, tables_s, db)))
    q_mla = jnp.transpose(ms["qmla"][0], (1, 0, 2))
    knew = ms["kcat"][0][:, None, :]
    o_lat = _mla_dec(pt_flat, q_mla, knew, cache_ckv, cache_kpet, layer, pp)
    fq = pad_heads(jnp.transpose(ms["fq"][0, :, :, 0:HEAD_DIM], (1, 0, 2)))
    lf_new = jnp.transpose(ms["logft"][0])[:, :, None]
    o_fox = _fox_dec(pt_flat, fq, tokmajor(ms["fkt"])[:, grp_rows], tokmajor(ms["fvt"])[:, grp_rows],
                     lf_new, cache_fkt, cache_fvt, cache_lft, layer, pp)
    mq = pad_heads(jnp.transpose(ms["mq"][0, :, :, 0:HEAD_DIM], (1, 0, 2)))
    o_moba = _moba_dec(pt_flat, mq, tokmajor(ms["mkt"])[:, grp_rows], tokmajor(ms["mvt"])[:, grp_rows],
                       cache_mkt, cache_mvt, layer, pp)
    return ms, o_lat, o_fox, o_moba


def kernel(x_prompt, x_sample, cache_mla_ckv, cache_mla_kpe, cache_fox_k, cache_fox_v, cache_fox_logf,
           cache_moba_k, cache_moba_v, page_table, p_prompt, p_sample, norm_mix_pre, norm_mix_post,
           norm_ffn_pre, norm_ffn_post, w_in, b_fox_f, mla_q_norm, mla_kv_norm, w_mla_uq, w_mla_uk,
           w_mla_uv, group_norm, w_out, w_ffn_gate, w_ffn_up, w_ffn_down, w_ple_proj, w_ple_gate):
    w = dict(norm_mix_pre=norm_mix_pre, norm_mix_post=norm_mix_post, norm_ffn_pre=norm_ffn_pre,
             norm_ffn_post=norm_ffn_post, w_in=w_in, b_fox_f=b_fox_f, mla_q_norm=mla_q_norm,
             mla_kv_norm=mla_kv_norm, w_mla_uq=w_mla_uq, w_mla_uk=w_mla_uk, w_mla_uv=w_mla_uv,
             group_norm=group_norm, w_out=w_out, w_ffn_gate=w_ffn_gate, w_ffn_up=w_ffn_up,
             w_ffn_down=w_ffn_down, w_ple_proj=w_ple_proj, w_ple_gate=w_ple_gate)
    depth = w_in.shape[0]
    bsz, seq, d = x_prompt.shape
    db = x_sample.shape[0]
    n_pages = page_table.shape[1]
    past = n_pages * PAGE_SIZE
    assert x_sample.shape[1] == 1 and seq % MOBA_BLOCK == 0 and past % MOBA_BLOCK == 0

    caches_t = _transposed_caches((cache_mla_ckv, cache_mla_kpe, cache_fox_k, cache_fox_v, cache_fox_logf,
                                   cache_moba_k, cache_moba_v))
    tables_p = _rope_tables(jnp.arange(seq))
    tq_proj = _pick(seq, (512, 256))
    tq_tail = _pick(seq, (512, 256, 128))
    th = _pick(w_ffn_gate.shape[2], (1408, 256, 128))
    tk_attn = _pick(seq, (512, 256))

    yp = x_prompt
    ys = x_sample.reshape(1, db, d)
    rows_p = [[] for _ in range(7)]
    rows_s = [[] for _ in range(7)]
    for i in range(depth):
        lw = _layer_weights(i, w)

        m = dict(zip(_PROJ_NAMES, _project(yp, lw, tables_p, tq_proj)))
        o_lat = _mla_attn(m["qmla"], m["kcat"], 128, tk_attn)
        o_fox = _fox_attn(m["fq"], m["fkt"], m["fvt"], m["cumt"], 256, tk_attn)
        o_moba = _moba_attn(m["mq"], m["mkt"], m["mvt"], m["kmt"], tk_attn)
        flat = lambda a: a.reshape(bsz * seq, a.shape[-1])
        x1, h2 = _mix(flat(o_lat), flat(o_fox), flat(o_moba), flat(yp), lw, tq_tail)
        yp = _ffn(x1, h2, flat(p_prompt[i]), lw, tq_tail, th).reshape(bsz, seq, d)
        for lst, r in zip(rows_p, _cache_rows(m)):
            lst.append(r)

        ms, o_lat_s, o_fox_s, o_moba_s = _sample_attention(ys, lw, i, page_table, caches_t)
        heads4 = lambda a: a[:, 0:4].reshape(db, 4 * HEAD_DIM)
        x1s, h2s = _mix(o_lat_s.reshape(db, MLA_HEADS * MLA_KV_LORA), heads4(o_fox_s), heads4(o_moba_s),
                        ys[0], lw, db)
        ys = _ffn(x1s, h2s, p_sample[i][:, 0, :], lw, db, th).reshape(1, db, d)
        for lst, r in zip(rows_s, _cache_rows(ms)):
            lst.append(jnp.transpose(r, (1, 0) + tuple(range(2, r.ndim))))

    outs = [yp, ys.reshape(db, 1, d)]
    for rp, rs in zip(rows_p, rows_s):
        outs.append(jnp.stack(rp))
        outs.append(jnp.stack(rs))
    return tuple(outs)
```
